```python
import jax, jax.numpy as jnp
from jax import lax
import numpy as np

D_MODEL = 1024
BATCH = 8
SEQ = 8192
DEPTH = 4

GRID_W = 64
CTX_LEN = 256
Q_BLOCK = 128
ROPE_THETA = 10000.0
NORM_EPS = 1e-6

GQA_HEADS = 6
GQA_KV_HEADS = 2
GQA_GROUP = GQA_HEADS // GQA_KV_HEADS
HEAD_DIM = 64
MLA_HEADS = 6
MLA_NOPE = 64
MLA_ROPE = 32
MLA_QK = MLA_NOPE + MLA_ROPE
MLA_V = 64
MLA_Q_RANK = 256
MLA_KV_RANK = 128
POOL_WINDOWS = (2, 4, 8, 16)
POOL_GROUPS = len(POOL_WINDOWS)
POOL_GROUP_DIM = 64
POOL_WIDTH = POOL_GROUPS * POOL_GROUP_DIM

MIX_WIDTH = GQA_HEADS * HEAD_DIM + MLA_HEADS * MLA_V + POOL_WIDTH
IN_SIZES = (GQA_HEADS * HEAD_DIM, GQA_KV_HEADS * HEAD_DIM, GQA_KV_HEADS * HEAD_DIM,
            MLA_Q_RANK, MLA_KV_RANK, MLA_ROPE, POOL_WIDTH)
IN_WIDTH = sum(IN_SIZES)
D_FF = 4 * D_MODEL
N_MOD = 6

kernel_name = "hybrid_headgroup_diffusion_prefix_trunk"


def rms_norm(x, g):
    xf = x.astype(jnp.float32)
    y = xf * lax.rsqrt(jnp.mean(xf * xf, axis=-1, keepdims=True) + NORM_EPS)
    return (y * g.astype(jnp.float32)).astype(x.dtype)


def modulate(h, shift, scale):
    return h * (1.0 + scale) + shift


def axial_rope_tables(length, rot_dim):
    rows = length // GRID_W
    row = jnp.repeat(jnp.arange(rows, dtype=jnp.float32), GRID_W)
    col = jnp.tile(jnp.arange(GRID_W, dtype=jnp.float32), rows)
    n = rot_dim // 4
    inv = ROPE_THETA ** (-jnp.arange(n, dtype=jnp.float32) / n)
    ang = jnp.concatenate([row[:, None] * inv, col[:, None] * inv], axis=-1)
    return jnp.cos(ang)[:, None, :], jnp.sin(ang)[:, None, :]


def apply_rope(x, cos, sin):
    half = x.shape[-1] // 2
    xf = x.astype(jnp.float32)
    x1, x2 = xf[..., :half], xf[..., half:]
    return jnp.concatenate([x1 * cos - x2 * sin, x1 * sin + x2 * cos], axis=-1).astype(x.dtype)


def attention(q, k, v, scale):
    b, hk, g, lq, dk = q.shape
    nb = lq // Q_BLOCK
    qb = jnp.moveaxis(q.reshape(b, hk, g, nb, Q_BLOCK, dk), 3, 0)

    def one_block(qi):
        s = jnp.einsum('bhgqd,bhkd->bhgqk', qi, k, preferred_element_type=jnp.float32) * scale
        p = jax.nn.softmax(s, axis=-1)
        return jnp.einsum('bhgqk,bhkd->bhgqd', p.astype(v.dtype), v)

    ob = lax.map(one_block, qb)
    return jnp.moveaxis(ob, 0, 3).reshape(b, hk, g, lq, v.shape[-1])


def merge_heads(o):
    b, hk, g, l, d = o.shape
    return o.transpose(0, 3, 1, 2, 4).reshape(b, l, hk * g * d)


def mixer_inputs(h, w_in, g_q_gqa, g_k_gqa, g_cq, g_ckv, w_uq, w_ukv, g_q_mla, g_k_mla, rope):
    b, l, _ = h.shape
    u = h @ w_in
    offs = np.cumsum(IN_SIZES)[:-1].tolist()
    u_q, u_k, u_v, u_cq, u_ckv, u_kr, pool_in = jnp.split(u, offs, axis=-1)
    qg = rms_norm(u_q.reshape(b, l, GQA_HEADS, HEAD_DIM), g_q_gqa)
    kg = rms_norm(u_k.reshape(b, l, GQA_KV_HEADS, HEAD_DIM), g_k_gqa)
    vg = u_v.reshape(b, l, GQA_KV_HEADS, HEAD_DIM)
    qm = (rms_norm(u_cq, g_cq) @ w_uq).reshape(b, l, MLA_HEADS, MLA_QK)
    kv = (rms_norm(u_ckv, g_ckv) @ w_ukv).reshape(b, l, MLA_HEADS, MLA_NOPE + MLA_V)
    k_rope = jnp.broadcast_to(u_kr[:, :, None, :], (b, l, MLA_HEADS, MLA_ROPE))
    km = jnp.concatenate([kv[..., :MLA_NOPE], k_rope], axis=-1)
    vm = kv[..., MLA_NOPE:]
    qm = rms_norm(qm, g_q_mla)
    km = rms_norm(km, g_k_mla)
    if rope is not None:
        cos_g, sin_g, cos_m, sin_m = rope
        qg = apply_rope(qg, cos_g, sin_g)
        kg = apply_rope(kg, cos_g, sin_g)
        qm = jnp.concatenate([qm[..., :MLA_NOPE], apply_rope(qm[..., MLA_NOPE:], cos_m, sin_m)], axis=-1)
        km = jnp.concatenate([km[..., :MLA_NOPE], apply_rope(km[..., MLA_NOPE:], cos_m, sin_m)], axis=-1)
    qg = qg.reshape(b, l, GQA_KV_HEADS, GQA_GROUP, HEAD_DIM).transpose(0, 2, 3, 1, 4)
    kg = kg.transpose(0, 2, 1, 3)
    vg = vg.transpose(0, 2, 1, 3)
    qm = qm.transpose(0, 2, 1, 3)[:, :, None]
    km = km.transpose(0, 2, 1, 3)
    vm = vm.transpose(0, 2, 1, 3)
    return qg, kg, vg, qm, km, vm, pool_in


def multiscale_pool(u, w_pool, ls_pool):
    b, l, _ = u.shape
    uf = u.astype(jnp.float32)
    cs = jnp.concatenate([jnp.zeros((b, 1, POOL_WIDTH), jnp.float32), jnp.cumsum(uf, axis=1)], axis=1)
    t = jnp.arange(l)
    outs = []
    for gi, w in enumerate(POOL_WINDOWS):
        lo = jnp.maximum(t - w // 2, 0)
        hi = jnp.minimum(t + w // 2, l)
        csg = cs[..., gi * POOL_GROUP_DIM:(gi + 1) * POOL_GROUP_DIM]
        s = jnp.take(csg, hi, axis=1) - jnp.take(csg, lo, axis=1)
        mean = s / (hi - lo).astype(jnp.float32)[None, :, None]
        outs.append(mean - uf[..., gi * POOL_GROUP_DIM:(gi + 1) * POOL_GROUP_DIM])
    y = jnp.stack(outs, axis=2).astype(u.dtype)
    y = jnp.einsum('blgc,gcd->blgd', y, w_pool).reshape(b, l, POOL_WIDTH)
    return y * ls_pool


def token_mix(qg, kg, vg, qm, km, vm, pool_in, w_pool, ls_pool, w_out):
    og = attention(qg, kg, vg, HEAD_DIM ** -0.5)
    om = attention(qm, km, vm, MLA_QK ** -0.5)
    op = multiscale_pool(pool_in, w_pool, ls_pool)
    return jnp.concatenate([merge_heads(og), merge_heads(om), op], axis=-1) @ w_out


def sq_relu_mlp(h, w1, w2):
    return jnp.square(jax.nn.relu(h @ w1)) @ w2


def setup_inputs(seed: int = 0) -> dict:
    key = jax.random.key(seed)
    ks = jax.random.split(key, 24)
    f32 = jnp.float32

    def dense(k, shape, fan_in, s=1.0):
        return jax.random.normal(k, shape, f32) * (s * fan_in ** -0.5)

    def gain(k, shape):
        return 1.0 + 0.1 * jax.random.normal(k, shape, f32)

    return {
        "x": jax.random.normal(ks[0], (BATCH, SEQ, D_MODEL), f32),
        "c": jax.random.normal(ks[1], (BATCH, D_MODEL), f32),
        "ctx": jax.random.normal(ks[2], (BATCH, CTX_LEN, D_MODEL), f32),
        "c_ctx": jax.random.normal(ks[3], (D_MODEL,), f32),
        "w_mod": dense(ks[4], (DEPTH, D_MODEL, N_MOD * D_MODEL), D_MODEL, 0.5),
        "b_mod": 0.02 * jax.random.normal(ks[5], (DEPTH, N_MOD * D_MODEL), f32),
        "g_norm1": gain(ks[6], (DEPTH, D_MODEL)),
        "g_norm2": gain(ks[7], (DEPTH, D_MODEL)),
        "w_in": dense(ks[8], (DEPTH, D_MODEL, IN_WIDTH), D_MODEL),
        "g_q_gqa": gain(ks[9], (DEPTH, HEAD_DIM)),
        "g_k_gqa": gain(ks[10], (DEPTH, HEAD_DIM)),
        "g_cq": gain(ks[11], (DEPTH, MLA_Q_RANK)),
        "g_ckv": gain(ks[12], (DEPTH, MLA_KV_RANK)),
        "w_uq": dense(ks[13], (DEPTH, MLA_Q_RANK, MLA_HEADS * MLA_QK), MLA_Q_RANK),
        "w_ukv": dense(ks[14], (DEPTH, MLA_KV_RANK, MLA_HEADS * (MLA_NOPE + MLA_V)), MLA_KV_RANK),
        "g_q_mla": gain(ks[15], (DEPTH, MLA_QK)),
        "g_k_mla": gain(ks[16], (DEPTH, MLA_QK)),
        "w_pool": dense(ks[17], (DEPTH, POOL_GROUPS, POOL_GROUP_DIM, POOL_GROUP_DIM), POOL_GROUP_DIM),
        "ls_pool": gain(ks[18], (DEPTH, POOL_WIDTH)),
        "w_out": dense(ks[19], (DEPTH, MIX_WIDTH, D_MODEL), MIX_WIDTH),
        "w_mlp1": dense(ks[20], (DEPTH, D_MODEL, D_FF), D_MODEL),
        "w_mlp2": dense(ks[21], (DEPTH, D_FF, D_MODEL), D_FF),
    }


def reference(x, c, ctx, c_ctx, w_mod, b_mod, g_norm1, g_norm2, w_in, g_q_gqa, g_k_gqa, g_cq, g_ckv,
              w_uq, w_ukv, g_q_mla, g_k_mla, w_pool, ls_pool, w_out, w_mlp1, w_mlp2):
    length = x.shape[1]
    cos_g, sin_g = axial_rope_tables(length, HEAD_DIM)
    cos_m, sin_m = axial_rope_tables(length, MLA_ROPE)
    rope = (cos_g.astype(x.dtype), sin_g.astype(x.dtype), cos_m.astype(x.dtype), sin_m.astype(x.dtype))
    x_lat, x_ctx = x, ctx
    for i in range(DEPTH):
        last = i == DEPTH - 1
        mod_lat = (jax.nn.silu(c) @ w_mod[i] + b_mod[i])[:, None, :]
        mod_ctx = (jax.nn.silu(c_ctx) @ w_mod[i] + b_mod[i])[None, None, :]
        sh1, sc1, gt1, sh2, sc2, gt2 = jnp.split(mod_lat, N_MOD, axis=-1)
        csh1, csc1, cgt1, csh2, csc2, cgt2 = jnp.split(mod_ctx, N_MOD, axis=-1)
        layer_w = (w_in[i], g_q_gqa[i], g_k_gqa[i], g_cq[i], g_ckv[i], w_uq[i], w_ukv[i], g_q_mla[i], g_k_mla[i])
        hc = modulate(rms_norm(x_ctx, g_norm1[i]), csh1, csc1)
        qg_c, kg_c, vg_c, qm_c, km_c, vm_c, pool_c = mixer_inputs(hc, *layer_w, None)
        h = modulate(rms_norm(x_lat, g_norm1[i]), sh1, sc1)
        qg, kg, vg, qm, km, vm, pool_l = mixer_inputs(h, *layer_w, rope)
        mix = token_mix(qg, jnp.concatenate([kg_c, kg], axis=2), jnp.concatenate([vg_c, vg], axis=2),
                        qm, jnp.concatenate([km_c, km], axis=2), jnp.concatenate([vm_c, vm], axis=2),
                        pool_l, w_pool[i], ls_pool[i], w_out[i])
        x_lat = x_lat + gt1 * mix
        x_lat = x_lat + gt2 * sq_relu_mlp(modulate(rms_norm(x_lat, g_norm2[i]), sh2, sc2), w_mlp1[i], w_mlp2[i])
        if not last:
            mix_c = token_mix(qg_c, kg_c, vg_c, qm_c, km_c, vm_c, pool_c, w_pool[i], ls_pool[i], w_out[i])
            x_ctx = x_ctx + cgt1 * mix_c
            x_ctx = x_ctx + cgt2 * sq_relu_mlp(modulate(rms_norm(x_ctx, g_norm2[i]), csh2, csc2),
                                               w_mlp1[i], w_mlp2[i])
    return x_lat
```

```python
import functools

import numpy as np
import jax
import jax.numpy as jnp
from jax import lax
from jax.experimental import pallas as pl
from jax.experimental.pallas import tpu as pltpu

F32 = jnp.float32
BF16 = jnp.bfloat16

GRID_W = 64
ROPE_THETA = 10000.0
NORM_EPS = 1e-6
GQA_HEADS = 6
GQA_KV_HEADS = 2
GQA_GROUP = GQA_HEADS // GQA_KV_HEADS
HEAD_DIM = 64
MLA_HEADS = 6
MLA_NOPE = 64
MLA_ROPE = 32
MLA_QK = MLA_NOPE + MLA_ROPE
MLA_V = 64
MLA_Q_RANK = 256
MLA_KV_RANK = 128
POOL_WINDOWS = (2, 4, 8, 16)
POOL_GROUP_DIM = 64
POOL_WIDTH = len(POOL_WINDOWS) * POOL_GROUP_DIM
N_MOD = 6

LANES = 128
QK_PAD = 128
TOKEN_TILE = 256
Q_TILE = 256
KV_CHUNK = 256
FF_CHUNK = 1024
VMEM_LIMIT = 56 * 1024 * 1024

_O_Q = 0
_O_QS = _O_Q + GQA_HEADS * HEAD_DIM
_O_K = _O_QS + GQA_HEADS * HEAD_DIM
_O_KS = _O_K + GQA_KV_HEADS * HEAD_DIM
_O_V = _O_KS + GQA_KV_HEADS * HEAD_DIM
_O_CQ = _O_V + GQA_KV_HEADS * HEAD_DIM
_O_CKV = _O_CQ + MLA_Q_RANK
_O_KR = _O_CKV + MLA_KV_RANK
_O_KRS = _O_KR + MLA_ROPE
_O_POOL = _O_KRS + MLA_ROPE
_IN_EXT = _O_POOL + POOL_WIDTH

_G_Q = 0
_G_QS = _G_Q + HEAD_DIM
_G_K = _G_QS + HEAD_DIM
_G_KS = _G_K + HEAD_DIM
_G_CQ = _G_KS + HEAD_DIM
_G_CKV = _G_CQ + MLA_Q_RANK
_G_QM_N = _G_CKV + MLA_KV_RANK
_G_QM_R = _G_QM_N + MLA_NOPE
_G_QM_RS = _G_QM_R + MLA_ROPE
_G_KM_N = _G_QM_RS + MLA_ROPE
_G_KM_R = _G_KM_N + MLA_NOPE
_G_KM_RS = _G_KM_R + MLA_ROPE
_G_ROWS = _G_KM_RS + MLA_ROPE

_R_CG = 0
_R_SG = _R_CG + HEAD_DIM
_R_CM = _R_SG + HEAD_DIM
_R_SM = _R_CM + MLA_ROPE
_R_ROWS = _R_SM + MLA_ROPE


def _wide(a, width):
    return jnp.concatenate([a] * (width // LANES), axis=1)


def _dot(a, b):
    return jnp.dot(a, b, preferred_element_type=F32)


def _mod_kernel(w_ref, c_ref, b_ref, o_ref):
    w = w_ref[0]
    b = b_ref[0]
    for j in range(c_ref.shape[0]):
        c = c_ref[j]
        s = c / (1.0 + jnp.exp(-c))
        o_ref[0, j, 0] = _dot(w, s.astype(BF16)) + b


def _mod_call(w_modT, c_rep, b_rep):
    depth, six_d, d = w_modT.shape
    nvec = c_rep.shape[0]
    return pl.pallas_call(
        _mod_kernel,
        grid=(depth, N_MOD),
        in_specs=[
            pl.BlockSpec((1, d, d), lambda i, n: (i, n, 0)),
            pl.BlockSpec((nvec, d, LANES), lambda i, n: (0, 0, 0)),
            pl.BlockSpec((1, d, LANES), lambda i, n: (i, n, 0)),
        ],
        out_specs=pl.BlockSpec((1, nvec, 1, d, LANES), lambda i, n: (i, 0, n, 0, 0)),
        out_shape=jax.ShapeDtypeStruct((depth, nvec, N_MOD, d, LANES), F32),
        compiler_params=pltpu.CompilerParams(
            dimension_semantics=("arbitrary", "arbitrary"), vmem_limit_bytes=VMEM_LIMIT),
        name="mod_vectors",
    )(w_modT, c_rep, b_rep)


def _rms_rows(a, n):
    return lax.rsqrt(jnp.sum(a * a, axis=0, keepdims=True) * (1.0 / n) + NORM_EPS)


def _proj_kernel(x_ref, sh_ref, sc_ref, gn_ref, w_ref, gp_ref, rope_ref, wuq_ref, wukv_ref,
                 qg_ref, kg_ref, vg_ref, qm_ref, km_ref, vm_ref, pool_ref):
    tm = x_ref.shape[2]
    wide = functools.partial(_wide, width=tm)
    x = x_ref[0]
    d_model = x.shape[0]
    xn = x * _rms_rows(x, d_model) * wide(gn_ref[0])
    h = xn * (1.0 + wide(sc_ref[0, 0, 0])) + wide(sh_ref[0, 0, 0])
    u = _dot(w_ref[0], h.astype(BF16))

    gp = gp_ref[0]

    def gain(off, n):
        return wide(gp[off:off + n])

    cg = rope_ref[_R_CG:_R_CG + HEAD_DIM]
    sg = rope_ref[_R_SG:_R_SG + HEAD_DIM]
    cm = rope_ref[_R_CM:_R_CM + MLA_ROPE]
    sm = rope_ref[_R_SM:_R_SM + MLA_ROPE]
    zeros_kv = jnp.zeros((HEAD_DIM, tm), F32)
    zeros_pad = jnp.zeros((QK_PAD - MLA_QK, tm), F32)

    scale_g = HEAD_DIM ** -0.5
    gcq = gain(_G_Q, HEAD_DIM) * cg * scale_g
    gsq = gain(_G_QS, HEAD_DIM) * sg * scale_g
    for hd in range(GQA_HEADS):
        a = u[_O_Q + hd * HEAD_DIM:_O_Q + (hd + 1) * HEAD_DIM]
        a_sw = u[_O_QS + hd * HEAD_DIM:_O_QS + (hd + 1) * HEAD_DIM]
        q = _rms_rows(a, HEAD_DIM) * (a * gcq + a_sw * gsq)
        parts = [zeros_kv] * GQA_KV_HEADS
        parts[hd // GQA_GROUP] = q
        qg_ref[0, hd] = jnp.concatenate(parts, axis=0).astype(BF16)
    gck = gain(_G_K, HEAD_DIM) * cg
    gsk = gain(_G_KS, HEAD_DIM) * sg
    ks = []
    for hd in range(GQA_KV_HEADS):
        a = u[_O_K + hd * HEAD_DIM:_O_K + (hd + 1) * HEAD_DIM]
        a_sw = u[_O_KS + hd * HEAD_DIM:_O_KS + (hd + 1) * HEAD_DIM]
        ks.append(_rms_rows(a, HEAD_DIM) * (a * gck + a_sw * gsk))
    kg_ref[0, 0] = jnp.concatenate(ks, axis=0).T.astype(BF16)
    for hd in range(GQA_KV_HEADS):
        vg_ref[0, hd] = u[_O_V + hd * HEAD_DIM:_O_V + (hd + 1) * HEAD_DIM].astype(BF16)

    cq = u[_O_CQ:_O_CQ + MLA_Q_RANK]
    cq_n = cq * _rms_rows(cq, MLA_Q_RANK) * gain(_G_CQ, MLA_Q_RANK)
    qm_all = _dot(wuq_ref[0], cq_n.astype(BF16))
    ckv = u[_O_CKV:_O_CKV + MLA_KV_RANK]
    ckv_n = ckv * _rms_rows(ckv, MLA_KV_RANK) * gain(_G_CKV, MLA_KV_RANK)
    kv_all = _dot(wukv_ref[0], ckv_n.astype(BF16))
    kr = u[_O_KR:_O_KR + MLA_ROPE]
    kr_sw = u[_O_KRS:_O_KRS + MLA_ROPE]
    kr_ss = jnp.sum(kr * kr, axis=0, keepdims=True)

    scale_m = MLA_QK ** -0.5
    gq_n = gain(_G_QM_N, MLA_NOPE) * scale_m
    gq_c = gain(_G_QM_R, MLA_ROPE) * cm * scale_m
    gq_s = gain(_G_QM_RS, MLA_ROPE) * sm * scale_m
    gk_n = gain(_G_KM_N, MLA_NOPE)
    gk_c = gain(_G_KM_R, MLA_ROPE) * cm
    gk_s = gain(_G_KM_RS, MLA_ROPE) * sm
    k_rope = kr * gk_c + kr_sw * gk_s
    sw_base = MLA_HEADS * MLA_QK
    for hd in range(MLA_HEADS):
        a = qm_all[hd * MLA_QK:(hd + 1) * MLA_QK]
        a_sw = qm_all[sw_base + hd * MLA_ROPE:sw_base + (hd + 1) * MLA_ROPE]
        r = _rms_rows(a, MLA_QK)
        a_n, a_r = a[:MLA_NOPE], a[MLA_NOPE:]
        q = jnp.concatenate([r * (a_n * gq_n), r * (a_r * gq_c + a_sw * gq_s), zeros_pad], axis=0)
        qm_ref[0, hd] = q.astype(BF16)

        kvh = kv_all[hd * (MLA_NOPE + MLA_V):(hd + 1) * (MLA_NOPE + MLA_V)]
        kn, v = kvh[:MLA_NOPE], kvh[MLA_NOPE:]
        rk = lax.rsqrt((jnp.sum(kn * kn, axis=0, keepdims=True) + kr_ss) * (1.0 / MLA_QK) + NORM_EPS)
        k = jnp.concatenate([rk * (kn * gk_n), rk * k_rope, zeros_pad], axis=0)
        km_ref[0, hd] = k.T.astype(BF16)
        vm_ref[0, hd] = v.astype(BF16)

    pool_ref[0] = u[_O_POOL:_O_POOL + POOL_WIDTH]


def _proj_call(layer, xT, modT, gn1, w_inT, gpack, ropeT, w_uqT, w_ukvT, n_ctx_tiles):
    b, d, t = xT.shape
    tm = TOKEN_TILE
    nt = t // tm
    nb = modT.shape[1] - 1

    def mod_spec(slot):
        return pl.BlockSpec((1, 1, 1, d, LANES),
                            lambda bi, ti: (layer, jnp.where(ti < n_ctx_tiles, nb, bi), slot, 0, 0))

    def const_spec(arr):
        return pl.BlockSpec((1,) + arr.shape[1:], lambda bi, ti: (layer,) + (0,) * (arr.ndim - 1))

    out_shapes = (
        jax.ShapeDtypeStruct((b, GQA_HEADS, QK_PAD, t), BF16),
        jax.ShapeDtypeStruct((b, 1, t, QK_PAD), BF16),
        jax.ShapeDtypeStruct((b, GQA_KV_HEADS, HEAD_DIM, t), BF16),
        jax.ShapeDtypeStruct((b, MLA_HEADS, QK_PAD, t), BF16),
        jax.ShapeDtypeStruct((b, MLA_HEADS, t, QK_PAD), BF16),
        jax.ShapeDtypeStruct((b, MLA_HEADS, MLA_V, t), BF16),
        jax.ShapeDtypeStruct((b, POOL_WIDTH, t), F32),
    )
    out_specs = (
        pl.BlockSpec((1, GQA_HEADS, QK_PAD, tm), lambda bi, ti: (bi, 0, 0, ti)),
        pl.BlockSpec((1, 1, tm, QK_PAD), lambda bi, ti: (bi, 0, ti, 0)),
        pl.BlockSpec((1, GQA_KV_HEADS, HEAD_DIM, tm), lambda bi, ti: (bi, 0, 0, ti)),
        pl.BlockSpec((1, MLA_HEADS, QK_PAD, tm), lambda bi, ti: (bi, 0, 0, ti)),
        pl.BlockSpec((1, MLA_HEADS, tm, QK_PAD), lambda bi, ti: (bi, 0, ti, 0)),
        pl.BlockSpec((1, MLA_HEADS, MLA_V, tm), lambda bi, ti: (bi, 0, 0, ti)),
        pl.BlockSpec((1, POOL_WIDTH, tm), lambda bi, ti: (bi, 0, ti)),
    )
    return pl.pallas_call(
        _proj_kernel,
        grid=(b, nt),
        in_specs=[
            pl.BlockSpec((1, d, tm), lambda bi, ti: (bi, 0, ti)),
            mod_spec(0), mod_spec(1),
            const_spec(gn1), const_spec(w_inT), const_spec(gpack),
            pl.BlockSpec((_R_ROWS, tm), lambda bi, ti: (0, ti)),
            const_spec(w_uqT), const_spec(w_ukvT),
        ],
        out_specs=out_specs,
        out_shape=out_shapes,
        compiler_params=pltpu.CompilerParams(
            dimension_semantics=("arbitrary", "arbitrary"), vmem_limit_bytes=VMEM_LIMIT),
        name="proj",
    )(xT, modT, modT, gn1, w_inT, gpack, ropeT, w_uqT, w_ukvT)


def _attn_kernel(q_ref, k_ref, v_ref, o_ref, *, n_ctx_tiles, n_ctx_chunks, n_chunks):
    tq = q_ref.shape[3]
    dv = v_ref.shape[2]
    q = q_ref[0, 0]

    def step(j, carry):
        m, l, acc = carry
        start = pl.multiple_of(j * KV_CHUNK, KV_CHUNK)
        s = _dot(k_ref[0, 0, pl.ds(start, KV_CHUNK), :], q)
        m_new = jnp.maximum(m, jnp.max(s, axis=0, keepdims=True))
        alpha = jnp.exp(m - m_new)
        p = jnp.exp(s - m_new)
        l = alpha * l + jnp.sum(p, axis=0, keepdims=True)
        acc = alpha * acc + _dot(v_ref[0, 0, :, pl.ds(start, KV_CHUNK)], p.astype(BF16))
        return m_new, l, acc

    n = jnp.where(pl.program_id(2) < n_ctx_tiles, n_ctx_chunks, n_chunks)
    init = (jnp.full((1, tq), -jnp.inf, F32), jnp.zeros((1, tq), F32), jnp.zeros((dv, tq), F32))
    _, l, acc = lax.fori_loop(0, n, step, init)
    o_ref[0] = (acc / l).astype(BF16)


def _attn_call(qT, k, vT, q_per_kv, shared_k, n_ctx_tiles, name):
    b, nh, _, t = qT.shape
    dv = vT.shape[2]
    tq = Q_TILE
    kern = functools.partial(_attn_kernel, n_ctx_tiles=n_ctx_tiles,
                             n_ctx_chunks=n_ctx_tiles * tq // KV_CHUNK, n_chunks=t // KV_CHUNK)
    k_map = (lambda bi, hi, qi: (bi, 0, 0, 0)) if shared_k else (lambda bi, hi, qi: (bi, hi, 0, 0))
    return pl.pallas_call(
        kern,
        grid=(b, nh, t // tq),
        in_specs=[
            pl.BlockSpec((1, 1, QK_PAD, tq), lambda bi, hi, qi: (bi, hi, 0, qi)),
            pl.BlockSpec((1, 1, t, QK_PAD), k_map),
            pl.BlockSpec((1, 1, dv, t), lambda bi, hi, qi: (bi, hi // q_per_kv, 0, 0)),
        ],
        out_specs=pl.BlockSpec((1, dv, tq), lambda bi, hi, qi: (bi, hi, qi)),
        out_shape=jax.ShapeDtypeStruct((b, nh * dv, t), BF16),
        compiler_params=pltpu.CompilerParams(
            dimension_semantics=("arbitrary", "arbitrary", "arbitrary"),
            vmem_limit_bytes=VMEM_LIMIT),
        name=name,
    )(qT, k, vT)


def _mix_mlp_kernel(x_ref, og_ref, om_ref, pp_ref, pc_ref, pn_ref, gt1_ref, sh2_ref, sc2_ref,
                    gt2_ref, gn_ref, ls_ref, wp_ref, wo_ref, w1_ref, w2_ref, o_ref,
                    *, n_ctx_tiles, n_tiles):
    tm = x_ref.shape[2]
    wide = functools.partial(_wide, width=tm)
    ti = pl.program_id(1)

    cur = pc_ref[0]
    prev_ok = jnp.logical_and(ti != 0, ti != n_ctx_tiles)
    next_ok = jnp.logical_and(ti != n_ctx_tiles - 1, ti != n_tiles - 1)
    prev = jnp.where(prev_ok, pp_ref[0][:, tm - LANES:], 0.0)
    nxt = jnp.where(next_ok, pn_ref[0][:, :LANES], 0.0)
    ext = jnp.concatenate([prev, cur, nxt], axis=1)
    width = tm + 2 * LANES

    def shifted(a, s):
        return pltpu.roll(a, s % width, axis=1)

    in_ctx = ti < n_ctx_tiles
    seg_start = jnp.where(in_ctx, 0, n_ctx_tiles * tm)
    seg_len = jnp.where(in_ctx, n_ctx_tiles * tm, (n_tiles - n_ctx_tiles) * tm)
    pos = lax.broadcasted_iota(jnp.int32, (1, tm), 1) + (ti * tm - seg_start)
    gd = POOL_GROUP_DIM
    run = ext + shifted(ext, 1)
    outs = []
    for gi, w in enumerate(POOL_WINDOWS):
        if gi > 0:
            half = w // 4
            run = run[gd:]
            run = shifted(run, half) + shifted(run, -half)
        cnt = jnp.minimum(pos + w // 2, seg_len) - jnp.maximum(pos - w // 2, 0)
        mean = run[:gd, LANES:LANES + tm] / cnt.astype(F32)
        dlt = (mean - cur[gi * gd:(gi + 1) * gd]).astype(BF16)
        outs.append(_dot(wp_ref[0, gi], dlt))
    op = jnp.concatenate(outs, axis=0) * wide(ls_ref[0])

    n_og = og_ref.shape[1]
    n_om = om_ref.shape[1]
    wo = wo_ref[0]
    y = (_dot(wo[:, :n_og], og_ref[0]) + _dot(wo[:, n_og:n_og + n_om], om_ref[0])
         + _dot(wo[:, n_og + n_om:], op.astype(BF16)))
    x1 = x_ref[0] + wide(gt1_ref[0, 0, 0]) * y

    d_model = x1.shape[0]
    xn = x1 * _rms_rows(x1, d_model) * wide(gn_ref[0])
    h2 = (xn * (1.0 + wide(sc2_ref[0, 0, 0])) + wide(sh2_ref[0, 0, 0])).astype(BF16)
    d_ff = w1_ref.shape[1]
    y2 = jnp.zeros_like(x1)
    for c in range(d_ff // FF_CHUNK):
        a = jnp.maximum(_dot(w1_ref[0, c * FF_CHUNK:(c + 1) * FF_CHUNK, :], h2), 0.0)
        y2 = y2 + _dot(w2_ref[0, :, c * FF_CHUNK:(c + 1) * FF_CHUNK], (a * a).astype(BF16))
    o_ref[0] = x1 + wide(gt2_ref[0, 0, 0]) * y2


def _mix_mlp_call(layer, xT, ogT, omT, poolT, modT, gn2, ls_rep, w_poolT, w_outT, w1T, w2T,
                  n_ctx_tiles):
    b, d, t = xT.shape
    tm = TOKEN_TILE
    nt = t // tm
    nb = modT.shape[1] - 1

    def mod_spec(slot):
        return pl.BlockSpec((1, 1, 1, d, LANES),
                            lambda bi, ti: (layer, jnp.where(ti < n_ctx_tiles, nb, bi), slot, 0, 0))

    def const_spec(arr):
        return pl.BlockSpec((1,) + arr.shape[1:], lambda bi, ti: (layer,) + (0,) * (arr.ndim - 1))

    def tok_spec(rows):
        return pl.BlockSpec((1, rows, tm), lambda bi, ti: (bi, 0, ti))

    kern = functools.partial(_mix_mlp_kernel, n_ctx_tiles=n_ctx_tiles, n_tiles=nt)
    return pl.pallas_call(
        kern,
        grid=(b, nt),
        in_specs=[
            tok_spec(d), tok_spec(ogT.shape[1]), tok_spec(omT.shape[1]),
            pl.BlockSpec((1, POOL_WIDTH, tm), lambda bi, ti: (bi, 0, jnp.maximum(ti - 1, 0))),
            tok_spec(POOL_WIDTH),
            pl.BlockSpec((1, POOL_WIDTH, tm), lambda bi, ti: (bi, 0, jnp.minimum(ti + 1, nt - 1))),
            mod_spec(2), mod_spec(3), mod_spec(4), mod_spec(5),
            const_spec(gn2), const_spec(ls_rep), const_spec(w_poolT), const_spec(w_outT),
            const_spec(w1T), const_spec(w2T),
        ],
        out_specs=tok_spec(d),
        out_shape=jax.ShapeDtypeStruct((b, d, t), F32),
        compiler_params=pltpu.CompilerParams(
            dimension_semantics=("arbitrary", "arbitrary"), vmem_limit_bytes=VMEM_LIMIT),
        name="mix_mlp",
    )(xT, ogT, omT, poolT, poolT, poolT, modT, modT, modT, modT, gn2, ls_rep, w_poolT, w_outT,
      w1T, w2T)


def _rope_table(seq_len, ctx_len):
    rows = seq_len // GRID_W
    row = jnp.repeat(jnp.arange(rows, dtype=F32), GRID_W)
    col = jnp.tile(jnp.arange(GRID_W, dtype=F32), rows)

    def block(rot_dim):
        n = rot_dim // 4
        inv = ROPE_THETA ** (-jnp.arange(n, dtype=F32) / n)
        ang = jnp.concatenate([row[:, None] * inv, col[:, None] * inv], axis=-1)
        cos, sin = jnp.cos(ang).T, jnp.sin(ang).T
        c = jnp.concatenate([cos, cos], axis=0)
        s = jnp.concatenate([-sin, sin], axis=0)
        c = jnp.concatenate([jnp.ones((rot_dim, ctx_len), F32), c], axis=1)
        s = jnp.concatenate([jnp.zeros((rot_dim, ctx_len), F32), s], axis=1)
        return c, s

    cg, sg = block(HEAD_DIM)
    cm, sm = block(MLA_ROPE)
    return jnp.concatenate([cg, sg, cm, sm], axis=0)


def _lane_rep(a):
    return jnp.broadcast_to(a[..., None], a.shape + (LANES,)).astype(F32)


def _swap_half(n):
    return (np.arange(n) + n // 2) % n


def _prep_weights(w_mod, b_mod, g_norm1, g_norm2, w_in, g_q_gqa, g_k_gqa, g_cq, g_ckv, w_uq, w_ukv,
                  g_q_mla, g_k_mla, w_pool, ls_pool, w_out, w_mlp1, w_mlp2):
    nq, nk = GQA_HEADS * HEAD_DIM, GQA_KV_HEADS * HEAD_DIM
    o_q, o_k, o_v = 0, nq, nq + nk
    o_cq = o_v + nk
    o_ckv = o_cq + MLA_Q_RANK
    o_kr = o_ckv + MLA_KV_RANK
    o_pool = o_kr + MLA_ROPE
    sw64 = _swap_half(HEAD_DIM)
    sw32 = _swap_half(MLA_ROPE)
    q_sw = np.concatenate([o_q + h * HEAD_DIM + sw64 for h in range(GQA_HEADS)])
    k_sw = np.concatenate([o_k + h * HEAD_DIM + sw64 for h in range(GQA_KV_HEADS)])
    cols = np.concatenate([
        np.arange(o_q, o_q + nq), q_sw, np.arange(o_k, o_k + nk), k_sw, np.arange(o_v, o_v + nk),
        np.arange(o_cq, o_cq + MLA_Q_RANK), np.arange(o_ckv, o_ckv + MLA_KV_RANK),
        np.arange(o_kr, o_kr + MLA_ROPE), o_kr + sw32, np.arange(o_pool, o_pool + POOL_WIDTH)])
    assert cols.shape[0] == _IN_EXT
    w_inT = jnp.transpose(w_in[:, :, cols], (0, 2, 1)).astype(BF16)

    uq_sw = np.concatenate([h * MLA_QK + MLA_NOPE + sw32 for h in range(MLA_HEADS)])
    uq_cols = np.concatenate([np.arange(MLA_HEADS * MLA_QK), uq_sw])
    w_uqT = jnp.transpose(w_uq[:, :, uq_cols], (0, 2, 1)).astype(BF16)
    w_ukvT = jnp.transpose(w_ukv, (0, 2, 1)).astype(BF16)

    gq_r, gk_r = g_q_mla[:, MLA_NOPE:], g_k_mla[:, MLA_NOPE:]
    gpack = jnp.concatenate([
        g_q_gqa, g_q_gqa[:, sw64], g_k_gqa, g_k_gqa[:, sw64], g_cq, g_ckv,
        g_q_mla[:, :MLA_NOPE], gq_r, gq_r[:, sw32], g_k_mla[:, :MLA_NOPE], gk_r, gk_r[:, sw32]], axis=1)
    assert gpack.shape[1] == _G_ROWS
    return dict(
        w_modT=jnp.transpose(w_mod, (0, 2, 1)).astype(BF16),
        b_rep=_lane_rep(b_mod),
        gn1=_lane_rep(g_norm1), gn2=_lane_rep(g_norm2),
        w_inT=w_inT, w_uqT=w_uqT, w_ukvT=w_ukvT, gpack=_lane_rep(gpack),
        w_poolT=jnp.transpose(w_pool, (0, 1, 3, 2)).astype(BF16),
        ls_rep=_lane_rep(ls_pool),
        w_outT=jnp.transpose(w_out, (0, 2, 1)).astype(BF16),
        w1T=jnp.transpose(w_mlp1, (0, 2, 1)).astype(BF16),
        w2T=jnp.transpose(w_mlp2, (0, 2, 1)).astype(BF16),
    )


def kernel(x, c, ctx, c_ctx, w_mod, b_mod, g_norm1, g_norm2, w_in, g_q_gqa, g_k_gqa, g_cq, g_ckv, w_uq, w_ukv, g_q_mla, g_k_mla, w_pool, ls_pool, w_out, w_mlp1, w_mlp2):
    batch, seq, d_model = x.shape
    ctx_len = ctx.shape[1]
    depth = w_in.shape[0]
    assert ctx_len % TOKEN_TILE == 0 and seq % TOKEN_TILE == 0 and seq % GRID_W == 0
    assert TOKEN_TILE == Q_TILE and Q_TILE % KV_CHUNK == 0
    n_ctx_tiles = ctx_len // TOKEN_TILE

    p = _prep_weights(w_mod, b_mod, g_norm1, g_norm2, w_in, g_q_gqa, g_k_gqa, g_cq, g_ckv, w_uq,
                      w_ukv, g_q_mla, g_k_mla, w_pool, ls_pool, w_out, w_mlp1, w_mlp2)
    ropeT = _rope_table(seq, ctx_len)
    c_rep = _lane_rep(jnp.concatenate([c, c_ctx[None, :]], axis=0))
    modT = _mod_call(p["w_modT"], c_rep, p["b_rep"])

    xT = jnp.transpose(jnp.concatenate([ctx, x], axis=1), (0, 2, 1))
    for i in range(depth):
        qg, kg, vg, qm, km, vm, pool_in = _proj_call(
            i, xT, modT, p["gn1"], p["w_inT"], p["gpack"], ropeT, p["w_uqT"], p["w_ukvT"], n_ctx_tiles)
        ogT = _attn_call(qg, kg, vg, GQA_GROUP, True, n_ctx_tiles, "attn_gqa")
        omT = _attn_call(qm, km, vm, 1, False, n_ctx_tiles, "attn_mla")
        xT = _mix_mlp_call(i, xT, ogT, omT, pool_in, modT, p["gn2"], p["ls_rep"], p["w_poolT"],
                           p["w_outT"], p["w1T"], p["w2T"], n_ctx_tiles)
    return jnp.transpose(xT[:, :, ctx_len:], (0, 2, 1))
```

```python
import functools
import math

import numpy as np
import jax
import jax.numpy as jnp
from jax import lax
from jax.experimental import pallas as pl
from jax.experimental.pallas import tpu as pltpu

F32 = jnp.float32
BF16 = jnp.bfloat16

GRID_W = 64
ROPE_THETA = 10000.0
NORM_EPS = 1e-6
GQA_HEADS = 6
GQA_KV_HEADS = 2
GQA_GROUP = GQA_HEADS // GQA_KV_HEADS
HEAD_DIM = 64
MLA_HEADS = 6
MLA_NOPE = 64
MLA_ROPE = 32
MLA_QK = MLA_NOPE + MLA_ROPE
MLA_V = 64
MLA_Q_RANK = 256
MLA_KV_RANK = 128
POOL_WINDOWS = (2, 4, 8, 16)
POOL_GROUP_DIM = 64
POOL_WIDTH = len(POOL_WINDOWS) * POOL_GROUP_DIM
N_MOD = 6

LANES = 128
QK_PAD = 128
TOKEN_TILE = 256
Q_TILE = 256
KV_CHUNK = 512
KV_CHUNK_ONLINE = 256
HEADS_PER_STEP = 3
V_ROWS = 80
FF_CHUNK = 1024
VMEM_LIMIT = 56 * 1024 * 1024
LOG2E = 1.4426950408889634
SAFE_LOG2_BOUND = 64.0

_O_Q = 0
_O_QS = _O_Q + GQA_HEADS * HEAD_DIM
_O_K = _O_QS + GQA_HEADS * HEAD_DIM
_O_KS = _O_K + GQA_KV_HEADS * HEAD_DIM
_O_V = _O_KS + GQA_KV_HEADS * HEAD_DIM
_O_CQ = _O_V + GQA_KV_HEADS * HEAD_DIM
_O_CKV = _O_CQ + MLA_Q_RANK
_O_KR = _O_CKV + MLA_KV_RANK
_O_KRS = _O_KR + MLA_ROPE
_O_POOL = _O_KRS + MLA_ROPE
_IN_EXT = _O_POOL + POOL_WIDTH

_G_Q = 0
_G_QS = _G_Q + HEAD_DIM
_G_K = _G_QS + HEAD_DIM
_G_KS = _G_K + HEAD_DIM
_G_CQ = _G_KS + HEAD_DIM
_G_CKV = _G_CQ + MLA_Q_RANK
_G_QM_N = _G_CKV + MLA_KV_RANK
_G_QM_R = _G_QM_N + MLA_NOPE
_G_QM_RS = _G_QM_R + MLA_ROPE
_G_KM_N = _G_QM_RS + MLA_ROPE
_G_KM_R = _G_KM_N + MLA_NOPE
_G_KM_RS = _G_KM_R + MLA_ROPE
_G_ROWS = _G_KM_RS + MLA_ROPE

_R_CG = 0
_R_SG = _R_CG + HEAD_DIM
_R_CM = _R_SG + HEAD_DIM
_R_SM = _R_CM + MLA_ROPE
_R_ROWS = _R_SM + MLA_ROPE


def _wide(a, width):
    return jnp.concatenate([a] * (width // LANES), axis=1)


def _dot(a, b):
    return jnp.dot(a, b, preferred_element_type=F32)


def _mod_kernel(w_ref, c_ref, b_ref, o_ref):
    w = w_ref[0]
    b = b_ref[0]
    for j in range(c_ref.shape[0]):
        c = c_ref[j]
        s = c / (1.0 + jnp.exp(-c))
        o_ref[0, j, 0] = _dot(w, s.astype(BF16)) + b


def _mod_call(w_modT, c_rep, b_rep):
    depth, six_d, d = w_modT.shape
    nvec = c_rep.shape[0]
    return pl.pallas_call(
        _mod_kernel,
        grid=(depth, N_MOD),
        in_specs=[
            pl.BlockSpec((1, d, d), lambda i, n: (i, n, 0)),
            pl.BlockSpec((nvec, d, LANES), lambda i, n: (0, 0, 0)),
            pl.BlockSpec((1, d, LANES), lambda i, n: (i, n, 0)),
        ],
        out_specs=pl.BlockSpec((1, nvec, 1, d, LANES), lambda i, n: (i, 0, n, 0, 0)),
        out_shape=jax.ShapeDtypeStruct((depth, nvec, N_MOD, d, LANES), F32),
        compiler_params=pltpu.CompilerParams(
            dimension_semantics=("arbitrary", "arbitrary"), vmem_limit_bytes=VMEM_LIMIT),
        name="mod_vectors",
    )(w_modT, c_rep, b_rep)


def _rms_rows(a, n):
    return lax.rsqrt(jnp.sum(a * a, axis=0, keepdims=True) * (1.0 / n) + NORM_EPS)


def _proj_kernel(x_ref, sh_ref, sc_ref, gn_ref, w_ref, gp_ref, rope_ref, wuq_ref, wukv_ref,
                 qg_ref, kg_ref, vg_ref, qm_ref, km_ref, vm_ref, pool_ref):
    tm = x_ref.shape[2]
    wide = functools.partial(_wide, width=tm)
    x = x_ref[0]
    d_model = x.shape[0]
    xn = x * _rms_rows(x, d_model) * wide(gn_ref[0])
    h = xn * (1.0 + wide(sc_ref[0, 0, 0])) + wide(sh_ref[0, 0, 0])
    u = _dot(w_ref[0], h.astype(BF16))

    gp = gp_ref[0]

    def gain(off, n):
        return wide(gp[off:off + n])

    cg = rope_ref[_R_CG:_R_CG + HEAD_DIM]
    sg = rope_ref[_R_SG:_R_SG + HEAD_DIM]
    cm = rope_ref[_R_CM:_R_CM + MLA_ROPE]
    sm = rope_ref[_R_SM:_R_SM + MLA_ROPE]
    zeros_kv = jnp.zeros((HEAD_DIM, tm), F32)
    zeros_pad = jnp.zeros((QK_PAD - MLA_QK, tm), F32)
    v_tail = (lax.broadcasted_iota(jnp.int32, (V_ROWS - HEAD_DIM, tm), 0) == 0).astype(F32)

    def with_ones_row(v):
        return jnp.concatenate([v, v_tail], axis=0).astype(BF16)

    scale_g = HEAD_DIM ** -0.5 * LOG2E
    gcq = gain(_G_Q, HEAD_DIM) * cg * scale_g
    gsq = gain(_G_QS, HEAD_DIM) * sg * scale_g
    for hd in range(GQA_HEADS):
        a = u[_O_Q + hd * HEAD_DIM:_O_Q + (hd + 1) * HEAD_DIM]
        a_sw = u[_O_QS + hd * HEAD_DIM:_O_QS + (hd + 1) * HEAD_DIM]
        q = _rms_rows(a, HEAD_DIM) * (a * gcq + a_sw * gsq)
        parts = [zeros_kv] * GQA_KV_HEADS
        parts[hd // GQA_GROUP] = q
        qg_ref[0, hd] = jnp.concatenate(parts, axis=0).astype(BF16)
    gck = gain(_G_K, HEAD_DIM) * cg
    gsk = gain(_G_KS, HEAD_DIM) * sg
    ks = []
    for hd in range(GQA_KV_HEADS):
        a = u[_O_K + hd * HEAD_DIM:_O_K + (hd + 1) * HEAD_DIM]
        a_sw = u[_O_KS + hd * HEAD_DIM:_O_KS + (hd + 1) * HEAD_DIM]
        ks.append(_rms_rows(a, HEAD_DIM) * (a * gck + a_sw * gsk))
    kg_ref[0, 0] = jnp.concatenate(ks, axis=0).T.astype(BF16)
    for hd in range(GQA_KV_HEADS):
        vg_ref[0, hd] = with_ones_row(u[_O_V + hd * HEAD_DIM:_O_V + (hd + 1) * HEAD_DIM])

    cq = u[_O_CQ:_O_CQ + MLA_Q_RANK]
    cq_n = cq * _rms_rows(cq, MLA_Q_RANK) * gain(_G_CQ, MLA_Q_RANK)
    qm_all = _dot(wuq_ref[0], cq_n.astype(BF16))
    ckv = u[_O_CKV:_O_CKV + MLA_KV_RANK]
    ckv_n = ckv * _rms_rows(ckv, MLA_KV_RANK) * gain(_G_CKV, MLA_KV_RANK)
    kv_all = _dot(wukv_ref[0], ckv_n.astype(BF16))
    kr = u[_O_KR:_O_KR + MLA_ROPE]
    kr_sw = u[_O_KRS:_O_KRS + MLA_ROPE]
    kr_ss = jnp.sum(kr * kr, axis=0, keepdims=True)

    scale_m = MLA_QK ** -0.5 * LOG2E
    gq_n = gain(_G_QM_N, MLA_NOPE) * scale_m
    gq_c = gain(_G_QM_R, MLA_ROPE) * cm * scale_m
    gq_s = gain(_G_QM_RS, MLA_ROPE) * sm * scale_m
    gk_n = gain(_G_KM_N, MLA_NOPE)
    gk_c = gain(_G_KM_R, MLA_ROPE) * cm
    gk_s = gain(_G_KM_RS, MLA_ROPE) * sm
    k_rope = kr * gk_c + kr_sw * gk_s
    sw_base = MLA_HEADS * MLA_QK
    for hd in range(MLA_HEADS):
        a = qm_all[hd * MLA_QK:(hd + 1) * MLA_QK]
        a_sw = qm_all[sw_base + hd * MLA_ROPE:sw_base + (hd + 1) * MLA_ROPE]
        r = _rms_rows(a, MLA_QK)
        a_n, a_r = a[:MLA_NOPE], a[MLA_NOPE:]
        q = jnp.concatenate([r * (a_n * gq_n), r * (a_r * gq_c + a_sw * gq_s), zeros_pad], axis=0)
        qm_ref[0, hd] = q.astype(BF16)

        kvh = kv_all[hd * (MLA_NOPE + MLA_V):(hd + 1) * (MLA_NOPE + MLA_V)]
        kn, v = kvh[:MLA_NOPE], kvh[MLA_NOPE:]
        rk = lax.rsqrt((jnp.sum(kn * kn, axis=0, keepdims=True) + kr_ss) * (1.0 / MLA_QK) + NORM_EPS)
        k = jnp.concatenate([rk * (kn * gk_n), rk * k_rope, zeros_pad], axis=0)
        km_ref[0, hd] = k.T.astype(BF16)
        vm_ref[0, hd] = with_ones_row(v)

    pool_ref[0] = u[_O_POOL:_O_POOL + POOL_WIDTH]


def _proj_call(layer, xT, modT, gn1, w_inT, gpack, ropeT, w_uqT, w_ukvT, n_ctx_tiles):
    b, d, t = xT.shape
    tm = TOKEN_TILE
    nt = t // tm
    nb = modT.shape[1] - 1

    def mod_spec(slot):
        return pl.BlockSpec((1, 1, 1, d, LANES),
                            lambda bi, ti: (layer, jnp.where(ti < n_ctx_tiles, nb, bi), slot, 0, 0))

    def const_spec(arr):
        return pl.BlockSpec((1,) + arr.shape[1:], lambda bi, ti: (layer,) + (0,) * (arr.ndim - 1))

    out_shapes = (
        jax.ShapeDtypeStruct((b, GQA_HEADS, QK_PAD, t), BF16),
        jax.ShapeDtypeStruct((b, 1, t, QK_PAD), BF16),
        jax.ShapeDtypeStruct((b, GQA_KV_HEADS, V_ROWS, t), BF16),
        jax.ShapeDtypeStruct((b, MLA_HEADS, QK_PAD, t), BF16),
        jax.ShapeDtypeStruct((b, MLA_HEADS, t, QK_PAD), BF16),
        jax.ShapeDtypeStruct((b, MLA_HEADS, V_ROWS, t), BF16),
        jax.ShapeDtypeStruct((b, POOL_WIDTH, t), F32),
    )
    out_specs = (
        pl.BlockSpec((1, GQA_HEADS, QK_PAD, tm), lambda bi, ti: (bi, 0, 0, ti)),
        pl.BlockSpec((1, 1, tm, QK_PAD), lambda bi, ti: (bi, 0, ti, 0)),
        pl.BlockSpec((1, GQA_KV_HEADS, V_ROWS, tm), lambda bi, ti: (bi, 0, 0, ti)),
        pl.BlockSpec((1, MLA_HEADS, QK_PAD, tm), lambda bi, ti: (bi, 0, 0, ti)),
        pl.BlockSpec((1, MLA_HEADS, tm, QK_PAD), lambda bi, ti: (bi, 0, ti, 0)),
        pl.BlockSpec((1, MLA_HEADS, V_ROWS, tm), lambda bi, ti: (bi, 0, 0, ti)),
        pl.BlockSpec((1, POOL_WIDTH, tm), lambda bi, ti: (bi, 0, ti)),
    )
    return pl.pallas_call(
        _proj_kernel,
        grid=(b, nt),
        in_specs=[
            pl.BlockSpec((1, d, tm), lambda bi, ti: (bi, 0, ti)),
            mod_spec(0), mod_spec(1),
            const_spec(gn1), const_spec(w_inT), const_spec(gpack),
            pl.BlockSpec((_R_ROWS, tm), lambda bi, ti: (0, ti)),
            const_spec(w_uqT), const_spec(w_ukvT),
        ],
        out_specs=out_specs,
        out_shape=out_shapes,
        compiler_params=pltpu.CompilerParams(
            dimension_semantics=("arbitrary", "arbitrary"), vmem_limit_bytes=VMEM_LIMIT),
        name="proj",
    )(xT, modT, modT, gn1, w_inT, gpack, ropeT, w_uqT, w_ukvT)


def _attn_kernel(flag_ref, q_ref, k_ref, v_ref, o_ref, acc_ref, *, shared_kv, n_ctx_tiles,
                 ctx_len, seq_len):
    heads = q_ref.shape[1]
    tq = q_ref.shape[3]
    is_ctx = pl.program_id(2) < n_ctx_tiles

    def kv(g):
        return 0 if shared_kv else g

    @pl.when(flag_ref[0] != 0)
    def _bounded_scores():
        def block(start, size, first):
            for g in range(heads):
                s = _dot(k_ref[0, kv(g), pl.ds(start, size), :], q_ref[0, g])
                pv = _dot(v_ref[0, kv(g), :, pl.ds(start, size)], jnp.exp2(s).astype(BF16))
                if first:
                    acc_ref[g] = pv
                else:
                    acc_ref[g] += pv

        block(0, ctx_len, True)
        align = math.gcd(ctx_len, KV_CHUNK)

        def step(j, carry):
            block(pl.multiple_of(ctx_len + j * KV_CHUNK, align), KV_CHUNK, False)
            return carry

        lax.fori_loop(0, jnp.where(is_ctx, 0, seq_len // KV_CHUNK), step, 0)

    @pl.when(flag_ref[0] == 0)
    def _online_softmax():
        chunk = KV_CHUNK_ONLINE
        n = jnp.where(is_ctx, ctx_len // chunk, (ctx_len + seq_len) // chunk)
        for g in range(heads):
            q = q_ref[0, g]

            def step(j, carry, g=g, q=q):
                m, acc = carry
                start = pl.multiple_of(j * chunk, chunk)
                s = _dot(k_ref[0, kv(g), pl.ds(start, chunk), :], q)
                m_new = jnp.maximum(m, jnp.max(s, axis=0, keepdims=True))
                p = jnp.exp2(s - m_new).astype(BF16)
                pv = _dot(v_ref[0, kv(g), :, pl.ds(start, chunk)], p)
                return m_new, jnp.exp2(m - m_new) * acc + pv

            init = (jnp.full((1, tq), -jnp.inf, F32), jnp.zeros((V_ROWS, tq), F32))
            acc_ref[g] = lax.fori_loop(0, n, step, init)[1]

    for g in range(heads):
        acc = acc_ref[g]
        o_ref[0, g * HEAD_DIM:(g + 1) * HEAD_DIM] = (
            acc[:HEAD_DIM] / acc[HEAD_DIM:HEAD_DIM + 1]).astype(BF16)


def _attn_call(flag, qT, k, vT, shared_kv, n_ctx_tiles, ctx_len, name):
    b, nh, _, t = qT.shape
    tq = Q_TILE
    g = HEADS_PER_STEP
    nkv = 1 if shared_kv else g
    kern = functools.partial(_attn_kernel, shared_kv=shared_kv, n_ctx_tiles=n_ctx_tiles,
                             ctx_len=ctx_len, seq_len=t - ctx_len)
    k_map = ((lambda bi, gi, qi, f: (bi, 0, 0, 0)) if shared_kv
             else (lambda bi, gi, qi, f: (bi, gi, 0, 0)))
    grid_spec = pltpu.PrefetchScalarGridSpec(
        num_scalar_prefetch=1,
        grid=(b, nh // g, t // tq),
        in_specs=[
            pl.BlockSpec((1, g, QK_PAD, tq), lambda bi, gi, qi, f: (bi, gi, 0, qi)),
            pl.BlockSpec((1, nkv, t, QK_PAD), k_map),
            pl.BlockSpec((1, nkv, V_ROWS, t), lambda bi, gi, qi, f: (bi, gi, 0, 0)),
        ],
        out_specs=pl.BlockSpec((1, g * HEAD_DIM, tq), lambda bi, gi, qi, f: (bi, gi, qi)),
        scratch_shapes=[pltpu.VMEM((g, V_ROWS, tq), F32)],
    )
    return pl.pallas_call(
        kern,
        grid_spec=grid_spec,
        out_shape=jax.ShapeDtypeStruct((b, nh * HEAD_DIM, t), BF16),
        compiler_params=pltpu.CompilerParams(
            dimension_semantics=("arbitrary", "arbitrary", "arbitrary"),
            vmem_limit_bytes=VMEM_LIMIT),
        name=name,
    )(flag, qT, k, vT)


def _mix_mlp_kernel(x_ref, og_ref, om_ref, pp_ref, pc_ref, pn_ref, gt1_ref, sh2_ref, sc2_ref,
                    gt2_ref, gn_ref, ls_ref, wp_ref, wo_ref, w1_ref, w2_ref, o_ref,
                    *, n_ctx_tiles, n_tiles):
    tm = x_ref.shape[2]
    wide = functools.partial(_wide, width=tm)
    ti = pl.program_id(1)

    cur = pc_ref[0]
    prev_ok = jnp.logical_and(ti != 0, ti != n_ctx_tiles)
    next_ok = jnp.logical_and(ti != n_ctx_tiles - 1, ti != n_tiles - 1)
    prev = jnp.where(prev_ok, pp_ref[0][:, tm - LANES:], 0.0)
    nxt = jnp.where(next_ok, pn_ref[0][:, :LANES], 0.0)
    ext = jnp.concatenate([prev, cur, nxt], axis=1)
    width = tm + 2 * LANES

    def shifted(a, s):
        return pltpu.roll(a, s % width, axis=1)

    in_ctx = ti < n_ctx_tiles
    seg_start = jnp.where(in_ctx, 0, n_ctx_tiles * tm)
    seg_len = jnp.where(in_ctx, n_ctx_tiles * tm, (n_tiles - n_ctx_tiles) * tm)
    pos = lax.broadcasted_iota(jnp.int32, (1, tm), 1) + (ti * tm - seg_start)
    gd = POOL_GROUP_DIM
    run = ext + shifted(ext, 1)
    outs = []
    for gi, w in enumerate(POOL_WINDOWS):
        if gi > 0:
            half = w // 4
            run = run[gd:]
            run = shifted(run, half) + shifted(run, -half)
        cnt = jnp.minimum(pos + w // 2, seg_len) - jnp.maximum(pos - w // 2, 0)
        mean = run[:gd, LANES:LANES + tm] / cnt.astype(F32)
        dlt = (mean - cur[gi * gd:(gi + 1) * gd]).astype(BF16)
        outs.append(_dot(wp_ref[0, gi], dlt))
    op = jnp.concatenate(outs, axis=0) * wide(ls_ref[0])

    n_og = og_ref.shape[1]
    n_om = om_ref.shape[1]
    wo = wo_ref[0]
    y = (_dot(wo[:, :n_og], og_ref[0]) + _dot(wo[:, n_og:n_og + n_om], om_ref[0])
         + _dot(wo[:, n_og + n_om:], op.astype(BF16)))
    x1 = x_ref[0] + wide(gt1_ref[0, 0, 0]) * y

    d_model = x1.shape[0]
    xn = x1 * _rms_rows(x1, d_model) * wide(gn_ref[0])
    h2 = (xn * (1.0 + wide(sc2_ref[0, 0, 0])) + wide(sh2_ref[0, 0, 0])).astype(BF16)
    d_ff = w1_ref.shape[1]
    y2 = jnp.zeros_like(x1)
    for c in range(d_ff // FF_CHUNK):
        a = jnp.maximum(_dot(w1_ref[0, c * FF_CHUNK:(c + 1) * FF_CHUNK, :], h2), 0.0)
        y2 = y2 + _dot(w2_ref[0, :, c * FF_CHUNK:(c + 1) * FF_CHUNK], (a * a).astype(BF16))
    o_ref[0] = x1 + wide(gt2_ref[0, 0, 0]) * y2


def _mix_mlp_call(layer, xT, ogT, omT, poolT, modT, gn2, ls_rep, w_poolT, w_outT, w1T, w2T,
                  n_ctx_tiles):
    b, d, t = xT.shape
    tm = TOKEN_TILE
    nt = t // tm
    nb = modT.shape[1] - 1

    def mod_spec(slot):
        return pl.BlockSpec((1, 1, 1, d, LANES),
                            lambda bi, ti: (layer, jnp.where(ti < n_ctx_tiles, nb, bi), slot, 0, 0))

    def const_spec(arr):
        return pl.BlockSpec((1,) + arr.shape[1:], lambda bi, ti: (layer,) + (0,) * (arr.ndim - 1))

    def tok_spec(rows):
        return pl.BlockSpec((1, rows, tm), lambda bi, ti: (bi, 0, ti))

    kern = functools.partial(_mix_mlp_kernel, n_ctx_tiles=n_ctx_tiles, n_tiles=nt)
    return pl.pallas_call(
        kern,
        grid=(b, nt),
        in_specs=[
            tok_spec(d), tok_spec(ogT.shape[1]), tok_spec(omT.shape[1]),
            pl.BlockSpec((1, POOL_WIDTH, tm), lambda bi, ti: (bi, 0, jnp.maximum(ti - 1, 0))),
            tok_spec(POOL_WIDTH),
            pl.BlockSpec((1, POOL_WIDTH, tm), lambda bi, ti: (bi, 0, jnp.minimum(ti + 1, nt - 1))),
            mod_spec(2), mod_spec(3), mod_spec(4), mod_spec(5),
            const_spec(gn2), const_spec(ls_rep), const_spec(w_poolT), const_spec(w_outT),
            const_spec(w1T), const_spec(w2T),
        ],
        out_specs=tok_spec(d),
        out_shape=jax.ShapeDtypeStruct((b, d, t), F32),
        compiler_params=pltpu.CompilerParams(
            dimension_semantics=("arbitrary", "arbitrary"), vmem_limit_bytes=VMEM_LIMIT),
        name="mix_mlp",
    )(xT, ogT, omT, poolT, poolT, poolT, modT, modT, modT, modT, gn2, ls_rep, w_poolT, w_outT,
      w1T, w2T)


def _rope_table(seq_len, ctx_len):
    rows = seq_len // GRID_W
    row = jnp.repeat(jnp.arange(rows, dtype=F32), GRID_W)
    col = jnp.tile(jnp.arange(GRID_W, dtype=F32), rows)

    def block(rot_dim):
        n = rot_dim // 4
        inv = ROPE_THETA ** (-jnp.arange(n, dtype=F32) / n)
        ang = jnp.concatenate([row[:, None] * inv, col[:, None] * inv], axis=-1)
        cos, sin = jnp.cos(ang).T, jnp.sin(ang).T
        c = jnp.concatenate([cos, cos], axis=0)
        s = jnp.concatenate([-sin, sin], axis=0)
        c = jnp.concatenate([jnp.ones((rot_dim, ctx_len), F32), c], axis=1)
        s = jnp.concatenate([jnp.zeros((rot_dim, ctx_len), F32), s], axis=1)
        return c, s

    cg, sg = block(HEAD_DIM)
    cm, sm = block(MLA_ROPE)
    return jnp.concatenate([cg, sg, cm, sm], axis=0)


def _lane_rep(a):
    return jnp.broadcast_to(a[..., None], a.shape + (LANES,)).astype(F32)


def _swap_half(n):
    return (np.arange(n) + n // 2) % n


def _prep_weights(w_mod, b_mod, g_norm1, g_norm2, w_in, g_q_gqa, g_k_gqa, g_cq, g_ckv, w_uq, w_ukv,
                  g_q_mla, g_k_mla, w_pool, ls_pool, w_out, w_mlp1, w_mlp2):
    nq, nk = GQA_HEADS * HEAD_DIM, GQA_KV_HEADS * HEAD_DIM
    o_q, o_k, o_v = 0, nq, nq + nk
    o_cq = o_v + nk
    o_ckv = o_cq + MLA_Q_RANK
    o_kr = o_ckv + MLA_KV_RANK
    o_pool = o_kr + MLA_ROPE
    sw64 = _swap_half(HEAD_DIM)
    sw32 = _swap_half(MLA_ROPE)
    q_sw = np.concatenate([o_q + h * HEAD_DIM + sw64 for h in range(GQA_HEADS)])
    k_sw = np.concatenate([o_k + h * HEAD_DIM + sw64 for h in range(GQA_KV_HEADS)])
    cols = np.concatenate([
        np.arange(o_q, o_q + nq), q_sw, np.arange(o_k, o_k + nk), k_sw, np.arange(o_v, o_v + nk),
        np.arange(o_cq, o_cq + MLA_Q_RANK), np.arange(o_ckv, o_ckv + MLA_KV_RANK),
        np.arange(o_kr, o_kr + MLA_ROPE), o_kr + sw32, np.arange(o_pool, o_pool + POOL_WIDTH)])
    assert cols.shape[0] == _IN_EXT
    w_inT = jnp.transpose(w_in[:, :, cols], (0, 2, 1)).astype(BF16)

    uq_sw = np.concatenate([h * MLA_QK + MLA_NOPE + sw32 for h in range(MLA_HEADS)])
    uq_cols = np.concatenate([np.arange(MLA_HEADS * MLA_QK), uq_sw])
    w_uqT = jnp.transpose(w_uq[:, :, uq_cols], (0, 2, 1)).astype(BF16)
    w_ukvT = jnp.transpose(w_ukv, (0, 2, 1)).astype(BF16)

    gq_r, gk_r = g_q_mla[:, MLA_NOPE:], g_k_mla[:, MLA_NOPE:]
    gpack = jnp.concatenate([
        g_q_gqa, g_q_gqa[:, sw64], g_k_gqa, g_k_gqa[:, sw64], g_cq, g_ckv,
        g_q_mla[:, :MLA_NOPE], gq_r, gq_r[:, sw32], g_k_mla[:, :MLA_NOPE], gk_r, gk_r[:, sw32]], axis=1)
    assert gpack.shape[1] == _G_ROWS
    return dict(
        w_modT=jnp.transpose(w_mod, (0, 2, 1)).astype(BF16),
        b_rep=_lane_rep(b_mod),
        gn1=_lane_rep(g_norm1), gn2=_lane_rep(g_norm2),
        w_inT=w_inT, w_uqT=w_uqT, w_ukvT=w_ukvT, gpack=_lane_rep(gpack),
        w_poolT=jnp.transpose(w_pool, (0, 1, 3, 2)).astype(BF16),
        ls_rep=_lane_rep(ls_pool),
        w_outT=jnp.transpose(w_out, (0, 2, 1)).astype(BF16),
        w1T=jnp.transpose(w_mlp1, (0, 2, 1)).astype(BF16),
        w2T=jnp.transpose(w_mlp2, (0, 2, 1)).astype(BF16),
    )


def kernel(x, c, ctx, c_ctx, w_mod, b_mod, g_norm1, g_norm2, w_in, g_q_gqa, g_k_gqa, g_cq, g_ckv, w_uq, w_ukv, g_q_mla, g_k_mla, w_pool, ls_pool, w_out, w_mlp1, w_mlp2):
    batch, seq, d_model = x.shape
    ctx_len = ctx.shape[1]
    depth = w_in.shape[0]
    assert ctx_len % TOKEN_TILE == 0 and seq % TOKEN_TILE == 0 and seq % GRID_W == 0
    assert TOKEN_TILE == Q_TILE and seq % KV_CHUNK == 0
    assert ctx_len % KV_CHUNK_ONLINE == 0 and seq % KV_CHUNK_ONLINE == 0
    assert GQA_GROUP == HEADS_PER_STEP and MLA_HEADS % HEADS_PER_STEP == 0
    n_ctx_tiles = ctx_len // TOKEN_TILE

    def bounded_flag(g_q, g_k, dim):
        bound = dim ** 0.5 * LOG2E * jnp.max(jnp.abs(g_q), axis=1) * jnp.max(jnp.abs(g_k), axis=1)
        return (bound <= SAFE_LOG2_BOUND).astype(jnp.int32)

    flag_g = bounded_flag(g_q_gqa, g_k_gqa, HEAD_DIM)
    flag_m = bounded_flag(g_q_mla, g_k_mla, MLA_QK)

    p = _prep_weights(w_mod, b_mod, g_norm1, g_norm2, w_in, g_q_gqa, g_k_gqa, g_cq, g_ckv, w_uq,
                      w_ukv, g_q_mla, g_k_mla, w_pool, ls_pool, w_out, w_mlp1, w_mlp2)
    ropeT = _rope_table(seq, ctx_len)
    c_rep = _lane_rep(jnp.concatenate([c, c_ctx[None, :]], axis=0))
    modT = _mod_call(p["w_modT"], c_rep, p["b_rep"])

    xT = jnp.transpose(jnp.concatenate([ctx, x], axis=1), (0, 2, 1))
    for i in range(depth):
        qg, kg, vg, qm, km, vm, pool_in = _proj_call(
            i, xT, modT, p["gn1"], p["w_inT"], p["gpack"], ropeT, p["w_uqT"], p["w_ukvT"], n_ctx_tiles)
        ogT = _attn_call(flag_g[i:i + 1], qg, kg, vg, True, n_ctx_tiles, ctx_len, "attn_gqa")
        omT = _attn_call(flag_m[i:i + 1], qm, km, vm, False, n_ctx_tiles, ctx_len, "attn_mla")
        xT = _mix_mlp_call(i, xT, ogT, omT, pool_in, modT, p["gn2"], p["ls_rep"], p["w_poolT"],
                           p["w_outT"], p["w1T"], p["w2T"], n_ctx_tiles)
    return jnp.transpose(xT[:, :, ctx_len:], (0, 2, 1))
```

```python
import functools
import math

import numpy as np
import jax
import jax.numpy as jnp
from jax import lax
from jax.experimental import pallas as pl
from jax.experimental.pallas import tpu as pltpu

F32 = jnp.float32
BF16 = jnp.bfloat16

GRID_W = 64
ROPE_THETA = 10000.0
NORM_EPS = 1e-6
GQA_HEADS = 6
GQA_KV_HEADS = 2
GQA_GROUP = GQA_HEADS // GQA_KV_HEADS
HEAD_DIM = 64
MLA_HEADS = 6
MLA_NOPE = 64
MLA_ROPE = 32
MLA_QK = MLA_NOPE + MLA_ROPE
MLA_V = 64
MLA_Q_RANK = 256
MLA_KV_RANK = 128
POOL_WINDOWS = (2, 4, 8, 16)
POOL_GROUP_DIM = 64
POOL_WIDTH = len(POOL_WINDOWS) * POOL_GROUP_DIM
N_MOD = 6

LANES = 128
QK_PAD = 128
TOKEN_TILE = 256
Q_TILE = 256
KV_CHUNK_MAX = 8448
KV_SUB = 256
CHAIN_SKEW = 4
KV_CHUNK_ONLINE = 256
HEADS_PER_STEP = 3
V_ROWS = 80
FF_CHUNK = 1024
VMEM_LIMIT = 56 * 1024 * 1024
LOG2E = 1.4426950408889634
SAFE_LOG2_BOUND = 64.0

_O_Q = 0
_O_QS = _O_Q + GQA_HEADS * HEAD_DIM
_O_K = _O_QS + GQA_HEADS * HEAD_DIM
_O_KS = _O_K + GQA_KV_HEADS * HEAD_DIM
_O_V = _O_KS + GQA_KV_HEADS * HEAD_DIM
_O_CQ = _O_V + GQA_KV_HEADS * HEAD_DIM
_O_CKV = _O_CQ + MLA_Q_RANK
_O_KR = _O_CKV + MLA_KV_RANK
_O_KRS = _O_KR + MLA_ROPE
_O_POOL = _O_KRS + MLA_ROPE
_IN_EXT = _O_POOL + POOL_WIDTH

_G_Q = 0
_G_QS = _G_Q + HEAD_DIM
_G_K = _G_QS + HEAD_DIM
_G_KS = _G_K + HEAD_DIM
_G_CQ = _G_KS + HEAD_DIM
_G_CKV = _G_CQ + MLA_Q_RANK
_G_QM_N = _G_CKV + MLA_KV_RANK
_G_QM_R = _G_QM_N + MLA_NOPE
_G_QM_RS = _G_QM_R + MLA_ROPE
_G_KM_N = _G_QM_RS + MLA_ROPE
_G_KM_R = _G_KM_N + MLA_NOPE
_G_KM_RS = _G_KM_R + MLA_ROPE
_G_ROWS = _G_KM_RS + MLA_ROPE

_R_CG = 0
_R_SG = _R_CG + HEAD_DIM
_R_CM = _R_SG + HEAD_DIM
_R_SM = _R_CM + MLA_ROPE
_R_ROWS = _R_SM + MLA_ROPE


def _wide(a, width):
    return jnp.concatenate([a] * (width // LANES), axis=1)


def _dot(a, b):
    return jnp.dot(a, b, preferred_element_type=F32)


def _mod_kernel(w_ref, c_ref, b_ref, o_ref):
    w = w_ref[0]
    b = b_ref[0]
    for j in range(c_ref.shape[0]):
        c = c_ref[j]
        s = c / (1.0 + jnp.exp(-c))
        o_ref[0, j, 0] = _dot(w, s.astype(BF16)) + b


def _mod_call(w_modT, c_rep, b_rep):
    depth, six_d, d = w_modT.shape
    nvec = c_rep.shape[0]
    return pl.pallas_call(
        _mod_kernel,
        grid=(depth, N_MOD),
        in_specs=[
            pl.BlockSpec((1, d, d), lambda i, n: (i, n, 0)),
            pl.BlockSpec((nvec, d, LANES), lambda i, n: (0, 0, 0)),
            pl.BlockSpec((1, d, LANES), lambda i, n: (i, n, 0)),
        ],
        out_specs=pl.BlockSpec((1, nvec, 1, d, LANES), lambda i, n: (i, 0, n, 0, 0)),
        out_shape=jax.ShapeDtypeStruct((depth, nvec, N_MOD, d, LANES), F32),
        compiler_params=pltpu.CompilerParams(
            dimension_semantics=("arbitrary", "arbitrary"), vmem_limit_bytes=VMEM_LIMIT),
        name="mod_vectors",
    )(w_modT, c_rep, b_rep)


def _rms_rows(a, n):
    return lax.rsqrt(jnp.sum(a * a, axis=0, keepdims=True) * (1.0 / n) + NORM_EPS)


def _proj_kernel(x_ref, sh_ref, sc_ref, gn_ref, w_ref, gp_ref, rope_ref, wuq_ref, wukv_ref,
                 qg_ref, kg_ref, vg_ref, qm_ref, km_ref, vm_ref, pool_ref):
    tm = x_ref.shape[2]
    wide = functools.partial(_wide, width=tm)
    x = x_ref[0]
    d_model = x.shape[0]
    xn = x * _rms_rows(x, d_model) * wide(gn_ref[0])
    h = xn * (1.0 + wide(sc_ref[0, 0, 0])) + wide(sh_ref[0, 0, 0])
    u = _dot(w_ref[0], h.astype(BF16))

    gp = gp_ref[0]

    def gain(off, n):
        return wide(gp[off:off + n])

    cg = rope_ref[_R_CG:_R_CG + HEAD_DIM]
    sg = rope_ref[_R_SG:_R_SG + HEAD_DIM]
    cm = rope_ref[_R_CM:_R_CM + MLA_ROPE]
    sm = rope_ref[_R_SM:_R_SM + MLA_ROPE]
    zeros_kv = jnp.zeros((HEAD_DIM, tm), F32)
    zeros_pad = jnp.zeros((QK_PAD - MLA_QK, tm), F32)
    v_tail = (lax.broadcasted_iota(jnp.int32, (V_ROWS - HEAD_DIM, tm), 0) == 0).astype(F32)

    def with_ones_row(v):
        return jnp.concatenate([v, v_tail], axis=0).astype(BF16)

    scale_g = HEAD_DIM ** -0.5 * LOG2E
    gcq = gain(_G_Q, HEAD_DIM) * cg * scale_g
    gsq = gain(_G_QS, HEAD_DIM) * sg * scale_g
    for hd in range(GQA_HEADS):
        a = u[_O_Q + hd * HEAD_DIM:_O_Q + (hd + 1) * HEAD_DIM]
        a_sw = u[_O_QS + hd * HEAD_DIM:_O_QS + (hd + 1) * HEAD_DIM]
        q = _rms_rows(a, HEAD_DIM) * (a * gcq + a_sw * gsq)
        parts = [zeros_kv] * GQA_KV_HEADS
        parts[hd // GQA_GROUP] = q
        qg_ref[0, hd] = jnp.concatenate(parts, axis=0).astype(BF16)
    gck = gain(_G_K, HEAD_DIM) * cg
    gsk = gain(_G_KS, HEAD_DIM) * sg
    ks = []
    for hd in range(GQA_KV_HEADS):
        a = u[_O_K + hd * HEAD_DIM:_O_K + (hd + 1) * HEAD_DIM]
        a_sw = u[_O_KS + hd * HEAD_DIM:_O_KS + (hd + 1) * HEAD_DIM]
        ks.append(_rms_rows(a, HEAD_DIM) * (a * gck + a_sw * gsk))
    kg_ref[0, 0] = jnp.concatenate(ks, axis=0).T.astype(BF16)
    for hd in range(GQA_KV_HEADS):
        vg_ref[0, hd] = with_ones_row(u[_O_V + hd * HEAD_DIM:_O_V + (hd + 1) * HEAD_DIM])

    cq = u[_O_CQ:_O_CQ + MLA_Q_RANK]
    cq_n = cq * _rms_rows(cq, MLA_Q_RANK) * gain(_G_CQ, MLA_Q_RANK)
    qm_all = _dot(wuq_ref[0], cq_n.astype(BF16))
    ckv = u[_O_CKV:_O_CKV + MLA_KV_RANK]
    ckv_n = ckv * _rms_rows(ckv, MLA_KV_RANK) * gain(_G_CKV, MLA_KV_RANK)
    kv_all = _dot(wukv_ref[0], ckv_n.astype(BF16))
    kr = u[_O_KR:_O_KR + MLA_ROPE]
    kr_sw = u[_O_KRS:_O_KRS + MLA_ROPE]
    kr_ss = jnp.sum(kr * kr, axis=0, keepdims=True)

    scale_m = MLA_QK ** -0.5 * LOG2E
    gq_n = gain(_G_QM_N, MLA_NOPE) * scale_m
    gq_c = gain(_G_QM_R, MLA_ROPE) * cm * scale_m
    gq_s = gain(_G_QM_RS, MLA_ROPE) * sm * scale_m
    gk_n = gain(_G_KM_N, MLA_NOPE)
    gk_c = gain(_G_KM_R, MLA_ROPE) * cm
    gk_s = gain(_G_KM_RS, MLA_ROPE) * sm
    k_rope = kr * gk_c + kr_sw * gk_s
    sw_base = MLA_HEADS * MLA_QK
    for hd in range(MLA_HEADS):
        a = qm_all[hd * MLA_QK:(hd + 1) * MLA_QK]
        a_sw = qm_all[sw_base + hd * MLA_ROPE:sw_base + (hd + 1) * MLA_ROPE]
        r = _rms_rows(a, MLA_QK)
        a_n, a_r = a[:MLA_NOPE], a[MLA_NOPE:]
        q = jnp.concatenate([r * (a_n * gq_n), r * (a_r * gq_c + a_sw * gq_s), zeros_pad], axis=0)
        qm_ref[0, hd] = q.astype(BF16)

        kvh = kv_all[hd * (MLA_NOPE + MLA_V):(hd + 1) * (MLA_NOPE + MLA_V)]
        kn, v = kvh[:MLA_NOPE], kvh[MLA_NOPE:]
        rk = lax.rsqrt((jnp.sum(kn * kn, axis=0, keepdims=True) + kr_ss) * (1.0 / MLA_QK) + NORM_EPS)
        k = jnp.concatenate([rk * (kn * gk_n), rk * k_rope, zeros_pad], axis=0)
        km_ref[0, hd] = k.T.astype(BF16)
        vm_ref[0, hd] = with_ones_row(v)

    pool_ref[0] = u[_O_POOL:_O_POOL + POOL_WIDTH]


def _proj_call(layer, xT, modT, gn1, w_inT, gpack, ropeT, w_uqT, w_ukvT, n_ctx_tiles):
    b, d, t = xT.shape
    tm = TOKEN_TILE
    nt = t // tm
    nb = modT.shape[1] - 1

    def mod_spec(slot):
        return pl.BlockSpec((1, 1, 1, d, LANES),
                            lambda bi, ti: (layer, jnp.where(ti < n_ctx_tiles, nb, bi), slot, 0, 0))

    def const_spec(arr):
        return pl.BlockSpec((1,) + arr.shape[1:], lambda bi, ti: (layer,) + (0,) * (arr.ndim - 1))

    out_shapes = (
        jax.ShapeDtypeStruct((b, GQA_HEADS, QK_PAD, t), BF16),
        jax.ShapeDtypeStruct((b, 1, t, QK_PAD), BF16),
        jax.ShapeDtypeStruct((b, GQA_KV_HEADS, V_ROWS, t), BF16),
        jax.ShapeDtypeStruct((b, MLA_HEADS, QK_PAD, t), BF16),
        jax.ShapeDtypeStruct((b, MLA_HEADS, t, QK_PAD), BF16),
        jax.ShapeDtypeStruct((b, MLA_HEADS, V_ROWS, t), BF16),
        jax.ShapeDtypeStruct((b, POOL_WIDTH, t), F32),
    )
    out_specs = (
        pl.BlockSpec((1, GQA_HEADS, QK_PAD, tm), lambda bi, ti: (bi, 0, 0, ti)),
        pl.BlockSpec((1, 1, tm, QK_PAD), lambda bi, ti: (bi, 0, ti, 0)),
        pl.BlockSpec((1, GQA_KV_HEADS, V_ROWS, tm), lambda bi, ti: (bi, 0, 0, ti)),
        pl.BlockSpec((1, MLA_HEADS, QK_PAD, tm), lambda bi, ti: (bi, 0, 0, ti)),
        pl.BlockSpec((1, MLA_HEADS, tm, QK_PAD), lambda bi, ti: (bi, 0, ti, 0)),
        pl.BlockSpec((1, MLA_HEADS, V_ROWS, tm), lambda bi, ti: (bi, 0, 0, ti)),
        pl.BlockSpec((1, POOL_WIDTH, tm), lambda bi, ti: (bi, 0, ti)),
    )
    return pl.pallas_call(
        _proj_kernel,
        grid=(b, nt),
        in_specs=[
            pl.BlockSpec((1, d, tm), lambda bi, ti: (bi, 0, ti)),
            mod_spec(0), mod_spec(1),
            const_spec(gn1), const_spec(w_inT), const_spec(gpack),
            pl.BlockSpec((_R_ROWS, tm), lambda bi, ti: (0, ti)),
            const_spec(w_uqT), const_spec(w_ukvT),
        ],
        out_specs=out_specs,
        out_shape=out_shapes,
        compiler_params=pltpu.CompilerParams(
            dimension_semantics=("arbitrary", "arbitrary"), vmem_limit_bytes=VMEM_LIMIT),
        name="proj",
    )(xT, modT, modT, gn1, w_inT, gpack, ropeT, w_uqT, w_ukvT)


def _attn_kernel(flag_ref, q_ref, k_ref, v_ref, o_ref, acc_ref, *, shared_kv, n_ctx_tiles,
                 ctx_len, seq_len, kv_chunk):
    heads = q_ref.shape[1]
    tq = q_ref.shape[3]
    is_ctx = pl.program_id(2) < n_ctx_tiles

    def kv(g):
        return 0 if shared_kv else g

    bounded = flag_ref[0] != 0

    def block(start, size, first):
        started = set()

        def finish_chain(g, lo, s):
            pv = _dot(v_ref[0, kv(g), :, pl.ds(lo, KV_SUB)], jnp.exp2(s).astype(BF16))
            if first and g not in started:
                acc_ref[g] = pv
            else:
                acc_ref[g] += pv
            started.add(g)

        pending = []
        for c in range(size // KV_SUB):
            lo = start + c * KV_SUB
            for g in range(heads):
                s = _dot(k_ref[0, kv(g), pl.ds(lo, KV_SUB), :], q_ref[0, g])
                pending.append((g, lo, s))
                if len(pending) > CHAIN_SKEW:
                    finish_chain(*pending.pop(0))
        for chain in pending:
            finish_chain(*chain)

    @pl.when(jnp.logical_and(bounded, is_ctx))
    def _bounded_scores_ctx():
        block(0, ctx_len, True)

    @pl.when(jnp.logical_and(bounded, jnp.logical_not(is_ctx)))
    def _bounded_scores_latent():
        block(0, kv_chunk, True)

        def step(j, carry):
            block(pl.multiple_of(j * kv_chunk, kv_chunk), kv_chunk, False)
            return carry

        lax.fori_loop(1, (ctx_len + seq_len) // kv_chunk, step, 0)

    @pl.when(jnp.logical_not(bounded))
    def _online_softmax():
        chunk = KV_CHUNK_ONLINE
        n = jnp.where(is_ctx, ctx_len // chunk, (ctx_len + seq_len) // chunk)
        for g in range(heads):
            q = q_ref[0, g]

            def step(j, carry, g=g, q=q):
                m, acc = carry
                start = pl.multiple_of(j * chunk, chunk)
                s = _dot(k_ref[0, kv(g), pl.ds(start, chunk), :], q)
                m_new = jnp.maximum(m, jnp.max(s, axis=0, keepdims=True))
                p = jnp.exp2(s - m_new).astype(BF16)
                pv = _dot(v_ref[0, kv(g), :, pl.ds(start, chunk)], p)
                return m_new, jnp.exp2(m - m_new) * acc + pv

            init = (jnp.full((1, tq), -jnp.inf, F32), jnp.zeros((V_ROWS, tq), F32))
            acc_ref[g] = lax.fori_loop(0, n, step, init)[1]

    for g in range(heads):
        acc = acc_ref[g]
        o_ref[0, g * HEAD_DIM:(g + 1) * HEAD_DIM] = (
            acc[:HEAD_DIM] / acc[HEAD_DIM:HEAD_DIM + 1]).astype(BF16)


def _kv_chunk(t):
    return max(c for c in range(Q_TILE, min(t, KV_CHUNK_MAX) + 1, Q_TILE) if t % c == 0)


def _attn_call(flag, qT, k, vT, shared_kv, n_ctx_tiles, ctx_len, name):
    b, nh, _, t = qT.shape
    tq = Q_TILE
    g = HEADS_PER_STEP
    nkv = 1 if shared_kv else g
    kern = functools.partial(_attn_kernel, shared_kv=shared_kv, n_ctx_tiles=n_ctx_tiles,
                             ctx_len=ctx_len, seq_len=t - ctx_len, kv_chunk=_kv_chunk(t))
    k_map = ((lambda bi, gi, qi, f: (bi, 0, 0, 0)) if shared_kv
             else (lambda bi, gi, qi, f: (bi, gi, 0, 0)))
    grid_spec = pltpu.PrefetchScalarGridSpec(
        num_scalar_prefetch=1,
        grid=(b, nh // g, t // tq),
        in_specs=[
            pl.BlockSpec((1, g, QK_PAD, tq), lambda bi, gi, qi, f: (bi, gi, 0, qi)),
            pl.BlockSpec((1, nkv, t, QK_PAD), k_map),
            pl.BlockSpec((1, nkv, V_ROWS, t), lambda bi, gi, qi, f: (bi, gi, 0, 0)),
        ],
        out_specs=pl.BlockSpec((1, g * HEAD_DIM, tq), lambda bi, gi, qi, f: (bi, gi, qi)),
        scratch_shapes=[pltpu.VMEM((g, V_ROWS, tq), F32)],
    )
    return pl.pallas_call(
        kern,
        grid_spec=grid_spec,
        out_shape=jax.ShapeDtypeStruct((b, nh * HEAD_DIM, t), BF16),
        compiler_params=pltpu.CompilerParams(
            dimension_semantics=("arbitrary", "arbitrary", "arbitrary"),
            vmem_limit_bytes=VMEM_LIMIT),
        name=name,
    )(flag, qT, k, vT)


def _mix_mlp_kernel(x_ref, og_ref, om_ref, pp_ref, pc_ref, pn_ref, gt1_ref, sh2_ref, sc2_ref,
                    gt2_ref, gn_ref, ls_ref, wp_ref, wo_ref, w1_ref, w2_ref, o_ref,
                    *, n_ctx_tiles, n_tiles):
    tm = x_ref.shape[2]
    wide = functools.partial(_wide, width=tm)
    ti = pl.program_id(1)

    cur = pc_ref[0]
    prev_ok = jnp.logical_and(ti != 0, ti != n_ctx_tiles)
    next_ok = jnp.logical_and(ti != n_ctx_tiles - 1, ti != n_tiles - 1)
    prev = jnp.where(prev_ok, pp_ref[0][:, tm - LANES:], 0.0)
    nxt = jnp.where(next_ok, pn_ref[0][:, :LANES], 0.0)
    ext = jnp.concatenate([prev, cur, nxt], axis=1)
    width = tm + 2 * LANES

    def shifted(a, s):
        return pltpu.roll(a, s % width, axis=1)

    in_ctx = ti < n_ctx_tiles
    seg_start = jnp.where(in_ctx, 0, n_ctx_tiles * tm)
    seg_len = jnp.where(in_ctx, n_ctx_tiles * tm, (n_tiles - n_ctx_tiles) * tm)
    pos = lax.broadcasted_iota(jnp.int32, (1, tm), 1) + (ti * tm - seg_start)
    gd = POOL_GROUP_DIM
    run = ext + shifted(ext, 1)
    outs = []
    for gi, w in enumerate(POOL_WINDOWS):
        if gi > 0:
            half = w // 4
            run = run[gd:]
            run = shifted(run, half) + shifted(run, -half)
        cnt = jnp.minimum(pos + w // 2, seg_len) - jnp.maximum(pos - w // 2, 0)
        mean = run[:gd, LANES:LANES + tm] / cnt.astype(F32)
        dlt = (mean - cur[gi * gd:(gi + 1) * gd]).astype(BF16)
        outs.append(_dot(wp_ref[0, gi], dlt))
    op = jnp.concatenate(outs, axis=0) * wide(ls_ref[0])

    n_og = og_ref.shape[1]
    n_om = om_ref.shape[1]
    wo = wo_ref[0]
    y = (_dot(wo[:, :n_og], og_ref[0]) + _dot(wo[:, n_og:n_og + n_om], om_ref[0])
         + _dot(wo[:, n_og + n_om:], op.astype(BF16)))
    x1 = x_ref[0] + wide(gt1_ref[0, 0, 0]) * y

    d_model = x1.shape[0]
    xn = x1 * _rms_rows(x1, d_model) * wide(gn_ref[0])
    h2 = (xn * (1.0 + wide(sc2_ref[0, 0, 0])) + wide(sh2_ref[0, 0, 0])).astype(BF16)
    d_ff = w1_ref.shape[1]
    y2 = jnp.zeros_like(x1)
    for c in range(d_ff // FF_CHUNK):
        a = jnp.maximum(_dot(w1_ref[0, c * FF_CHUNK:(c + 1) * FF_CHUNK, :], h2), 0.0)
        y2 = y2 + _dot(w2_ref[0, :, c * FF_CHUNK:(c + 1) * FF_CHUNK], (a * a).astype(BF16))
    o_ref[0] = x1 + wide(gt2_ref[0, 0, 0]) * y2


def _mix_mlp_call(layer, xT, ogT, omT, poolT, modT, gn2, ls_rep, w_poolT, w_outT, w1T, w2T,
                  n_ctx_tiles):
    b, d, t = xT.shape
    tm = TOKEN_TILE
    nt = t // tm
    nb = modT.shape[1] - 1

    def mod_spec(slot):
        return pl.BlockSpec((1, 1, 1, d, LANES),
                            lambda bi, ti: (layer, jnp.where(ti < n_ctx_tiles, nb, bi), slot, 0, 0))

    def const_spec(arr):
        return pl.BlockSpec((1,) + arr.shape[1:], lambda bi, ti: (layer,) + (0,) * (arr.ndim - 1))

    def tok_spec(rows):
        return pl.BlockSpec((1, rows, tm), lambda bi, ti: (bi, 0, ti))

    kern = functools.partial(_mix_mlp_kernel, n_ctx_tiles=n_ctx_tiles, n_tiles=nt)
    return pl.pallas_call(
        kern,
        grid=(b, nt),
        in_specs=[
            tok_spec(d), tok_spec(ogT.shape[1]), tok_spec(omT.shape[1]),
            pl.BlockSpec((1, POOL_WIDTH, tm), lambda bi, ti: (bi, 0, jnp.maximum(ti - 1, 0))),
            tok_spec(POOL_WIDTH),
            pl.BlockSpec((1, POOL_WIDTH, tm), lambda bi, ti: (bi, 0, jnp.minimum(ti + 1, nt - 1))),
            mod_spec(2), mod_spec(3), mod_spec(4), mod_spec(5),
            const_spec(gn2), const_spec(ls_rep), const_spec(w_poolT), const_spec(w_outT),
            const_spec(w1T), const_spec(w2T),
        ],
        out_specs=tok_spec(d),
        out_shape=jax.ShapeDtypeStruct((b, d, t), F32),
        compiler_params=pltpu.CompilerParams(
            dimension_semantics=("arbitrary", "arbitrary"), vmem_limit_bytes=VMEM_LIMIT),
        name="mix_mlp",
    )(xT, ogT, omT, poolT, poolT, poolT, modT, modT, modT, modT, gn2, ls_rep, w_poolT, w_outT,
      w1T, w2T)


def _rope_table(seq_len, ctx_len):
    rows = seq_len // GRID_W
    row = jnp.repeat(jnp.arange(rows, dtype=F32), GRID_W)
    col = jnp.tile(jnp.arange(GRID_W, dtype=F32), rows)

    def block(rot_dim):
        n = rot_dim // 4
        inv = ROPE_THETA ** (-jnp.arange(n, dtype=F32) / n)
        ang = jnp.concatenate([row[:, None] * inv, col[:, None] * inv], axis=-1)
        cos, sin = jnp.cos(ang).T, jnp.sin(ang).T
        c = jnp.concatenate([cos, cos], axis=0)
        s = jnp.concatenate([-sin, sin], axis=0)
        c = jnp.concatenate([jnp.ones((rot_dim, ctx_len), F32), c], axis=1)
        s = jnp.concatenate([jnp.zeros((rot_dim, ctx_len), F32), s], axis=1)
        return c, s

    cg, sg = block(HEAD_DIM)
    cm, sm = block(MLA_ROPE)
    return jnp.concatenate([cg, sg, cm, sm], axis=0)


def _lane_rep(a):
    return jnp.broadcast_to(a[..., None], a.shape + (LANES,)).astype(F32)


def _swap_half(n):
    return (np.arange(n) + n // 2) % n


def _prep_weights(w_mod, b_mod, g_norm1, g_norm2, w_in, g_q_gqa, g_k_gqa, g_cq, g_ckv, w_uq, w_ukv,
                  g_q_mla, g_k_mla, w_pool, ls_pool, w_out, w_mlp1, w_mlp2):
    nq, nk = GQA_HEADS * HEAD_DIM, GQA_KV_HEADS * HEAD_DIM
    o_q, o_k, o_v = 0, nq, nq + nk
    o_cq = o_v + nk
    o_ckv = o_cq + MLA_Q_RANK
    o_kr = o_ckv + MLA_KV_RANK
    o_pool = o_kr + MLA_ROPE
    sw64 = _swap_half(HEAD_DIM)
    sw32 = _swap_half(MLA_ROPE)
    q_sw = np.concatenate([o_q + h * HEAD_DIM + sw64 for h in range(GQA_HEADS)])
    k_sw = np.concatenate([o_k + h * HEAD_DIM + sw64 for h in range(GQA_KV_HEADS)])
    cols = np.concatenate([
        np.arange(o_q, o_q + nq), q_sw, np.arange(o_k, o_k + nk), k_sw, np.arange(o_v, o_v + nk),
        np.arange(o_cq, o_cq + MLA_Q_RANK), np.arange(o_ckv, o_ckv + MLA_KV_RANK),
        np.arange(o_kr, o_kr + MLA_ROPE), o_kr + sw32, np.arange(o_pool, o_pool + POOL_WIDTH)])
    assert cols.shape[0] == _IN_EXT
    w_inT = jnp.transpose(w_in[:, :, cols], (0, 2, 1)).astype(BF16)

    uq_sw = np.concatenate([h * MLA_QK + MLA_NOPE + sw32 for h in range(MLA_HEADS)])
    uq_cols = np.concatenate([np.arange(MLA_HEADS * MLA_QK), uq_sw])
    w_uqT = jnp.transpose(w_uq[:, :, uq_cols], (0, 2, 1)).astype(BF16)
    w_ukvT = jnp.transpose(w_ukv, (0, 2, 1)).astype(BF16)

    gq_r, gk_r = g_q_mla[:, MLA_NOPE:], g_k_mla[:, MLA_NOPE:]
    gpack = jnp.concatenate([
        g_q_gqa, g_q_gqa[:, sw64], g_k_gqa, g_k_gqa[:, sw64], g_cq, g_ckv,
        g_q_mla[:, :MLA_NOPE], gq_r, gq_r[:, sw32], g_k_mla[:, :MLA_NOPE], gk_r, gk_r[:, sw32]], axis=1)
    assert gpack.shape[1] == _G_ROWS
    return dict(
        w_modT=jnp.transpose(w_mod, (0, 2, 1)).astype(BF16),
        b_rep=_lane_rep(b_mod),
        gn1=_lane_rep(g_norm1), gn2=_lane_rep(g_norm2),
        w_inT=w_inT, w_uqT=w_uqT, w_ukvT=w_ukvT, gpack=_lane_rep(gpack),
        w_poolT=jnp.transpose(w_pool, (0, 1, 3, 2)).astype(BF16),
        ls_rep=_lane_rep(ls_pool),
        w_outT=jnp.transpose(w_out, (0, 2, 1)).astype(BF16),
        w1T=jnp.transpose(w_mlp1, (0, 2, 1)).astype(BF16),
        w2T=jnp.transpose(w_mlp2, (0, 2, 1)).astype(BF16),
    )


def kernel(x, c, ctx, c_ctx, w_mod, b_mod, g_norm1, g_norm2, w_in, g_q_gqa, g_k_gqa, g_cq, g_ckv, w_uq, w_ukv, g_q_mla, g_k_mla, w_pool, ls_pool, w_out, w_mlp1, w_mlp2):
    batch, seq, d_model = x.shape
    ctx_len = ctx.shape[1]
    depth = w_in.shape[0]
    assert ctx_len % TOKEN_TILE == 0 and seq % TOKEN_TILE == 0 and seq % GRID_W == 0
    assert TOKEN_TILE == Q_TILE
    assert ctx_len % KV_CHUNK_ONLINE == 0 and seq % KV_CHUNK_ONLINE == 0
    assert GQA_GROUP == HEADS_PER_STEP and MLA_HEADS % HEADS_PER_STEP == 0
    n_ctx_tiles = ctx_len // TOKEN_TILE

    def bounded_flag(g_q, g_k, dim):
        bound = dim ** 0.5 * LOG2E * jnp.max(jnp.abs(g_q), axis=1) * jnp.max(jnp.abs(g_k), axis=1)
        return (bound <= SAFE_LOG2_BOUND).astype(jnp.int32)

    flag_g = bounded_flag(g_q_gqa, g_k_gqa, HEAD_DIM)
    flag_m = bounded_flag(g_q_mla, g_k_mla, MLA_QK)

    p = _prep_weights(w_mod, b_mod, g_norm1, g_norm2, w_in, g_q_gqa, g_k_gqa, g_cq, g_ckv, w_uq,
                      w_ukv, g_q_mla, g_k_mla, w_pool, ls_pool, w_out, w_mlp1, w_mlp2)
    ropeT = _rope_table(seq, ctx_len)
    c_rep = _lane_rep(jnp.concatenate([c, c_ctx[None, :]], axis=0))
    modT = _mod_call(p["w_modT"], c_rep, p["b_rep"])

    xT = jnp.transpose(jnp.concatenate([ctx, x], axis=1), (0, 2, 1))
    for i in range(depth):
        qg, kg, vg, qm, km, vm, pool_in = _proj_call(
            i, xT, modT, p["gn1"], p["w_inT"], p["gpack"], ropeT, p["w_uqT"], p["w_ukvT"], n_ctx_tiles)
        ogT = _attn_call(flag_g[i:i + 1], qg, kg, vg, True, n_ctx_tiles, ctx_len, "attn_gqa")
        omT = _attn_call(flag_m[i:i + 1], qm, km, vm, False, n_ctx_tiles, ctx_len, "attn_mla")
        xT = _mix_mlp_call(i, xT, ogT, omT, pool_in, modT, p["gn2"], p["ls_rep"], p["w_poolT"],
                           p["w_outT"], p["w1T"], p["w2T"], n_ctx_tiles)
    return jnp.transpose(xT[:, :, ctx_len:], (0, 2, 1))
```

```python
import functools
import math

import numpy as np
import jax
import jax.numpy as jnp
from jax import lax
from jax.experimental import pallas as pl
from jax.experimental.pallas import tpu as pltpu

F32 = jnp.float32
BF16 = jnp.bfloat16

GRID_W = 64
ROPE_THETA = 10000.0
NORM_EPS = 1e-6
GQA_HEADS = 6
GQA_KV_HEADS = 2
GQA_GROUP = GQA_HEADS // GQA_KV_HEADS
HEAD_DIM = 64
MLA_HEADS = 6
MLA_NOPE = 64
MLA_ROPE = 32
MLA_QK = MLA_NOPE + MLA_ROPE
MLA_V = 64
MLA_Q_RANK = 256
MLA_KV_RANK = 128
POOL_WINDOWS = (2, 4, 8, 16)
POOL_GROUP_DIM = 64
POOL_WIDTH = len(POOL_WINDOWS) * POOL_GROUP_DIM
N_MOD = 6

LANES = 128
QK_PAD = 128
TOKEN_TILE = 256
Q_TILE = 256
KV_CHUNK_MAX = 8448
KV_SUB = 256
CHAIN_SKEW = 5
KV_CHUNK_ONLINE = 256
HEADS_PER_STEP = 3
V_ROWS = HEAD_DIM
SUBLANES = 8
FF_CHUNK = 256
VMEM_LIMIT = 56 * 1024 * 1024
LOG2E = 1.4426950408889634
SAFE_LOG2_BOUND = 64.0

_L_CQ = 0
_L_CKV = _L_CQ + MLA_Q_RANK
_L_KR = _L_CKV + MLA_KV_RANK
_L_KRS = _L_KR + MLA_ROPE
_N_LAT = _L_KRS + MLA_ROPE
_O_Q = 0
_O_QS = _O_Q + GQA_HEADS * HEAD_DIM
_O_K = _O_QS + GQA_HEADS * HEAD_DIM
_O_KS = _O_K + GQA_KV_HEADS * HEAD_DIM
_O_V = _O_KS + GQA_KV_HEADS * HEAD_DIM
_O_POOL = _O_V + GQA_KV_HEADS * HEAD_DIM
_IN_EXT = _N_LAT + _O_POOL + POOL_WIDTH

_G_Q = 0
_G_QS = _G_Q + HEAD_DIM
_G_K = _G_QS + HEAD_DIM
_G_KS = _G_K + HEAD_DIM
_G_CQ = _G_KS + HEAD_DIM
_G_CKV = _G_CQ + MLA_Q_RANK
_G_QM_N = _G_CKV + MLA_KV_RANK
_G_QM_R = _G_QM_N + MLA_NOPE
_G_QM_RS = _G_QM_R + MLA_ROPE
_G_KM_N = _G_QM_RS + MLA_ROPE
_G_KM_R = _G_KM_N + MLA_NOPE
_G_KM_RS = _G_KM_R + MLA_ROPE
_G_ROWS = _G_KM_RS + MLA_ROPE

_R_CG = 0
_R_SG = _R_CG + HEAD_DIM
_R_CM = _R_SG + HEAD_DIM
_R_SM = _R_CM + MLA_ROPE
_R_ROWS = _R_SM + MLA_ROPE


def _wide(a, width):
    return jnp.concatenate([a] * (width // LANES), axis=1)


def _dot(a, b):
    return jnp.dot(a, b, preferred_element_type=F32)


def _mod_kernel(w_ref, c_ref, b_ref, o_ref):
    w = w_ref[0]
    b = b_ref[0]
    for j in range(c_ref.shape[0]):
        c = c_ref[j]
        s = c / (1.0 + jnp.exp(-c))
        o_ref[0, j, 0] = _dot(w, s.astype(BF16)) + b


def _mod_call(w_modT, c_rep, b_rep):
    depth, six_d, d = w_modT.shape
    nvec = c_rep.shape[0]
    return pl.pallas_call(
        _mod_kernel,
        grid=(depth, N_MOD),
        in_specs=[
            pl.BlockSpec((1, d, d), lambda i, n: (i, n, 0)),
            pl.BlockSpec((nvec, d, LANES), lambda i, n: (0, 0, 0)),
            pl.BlockSpec((1, d, LANES), lambda i, n: (i, n, 0)),
        ],
        out_specs=pl.BlockSpec((1, nvec, 1, d, LANES), lambda i, n: (i, 0, n, 0, 0)),
        out_shape=jax.ShapeDtypeStruct((depth, nvec, N_MOD, d, LANES), F32),
        compiler_params=pltpu.CompilerParams(
            dimension_semantics=("arbitrary", "arbitrary"), vmem_limit_bytes=VMEM_LIMIT),
        name="mod_vectors",
    )(w_modT, c_rep, b_rep)


def _rms_rows(a, n):
    return lax.rsqrt(jnp.sum(a * a, axis=0, keepdims=True) * (1.0 / n) + NORM_EPS)


def _proj_kernel(x_ref, sh_ref, sc_ref, gn_ref, w_ref, gp_ref, rope_ref, wuq_ref, wukv_ref,
                 qg_ref, kg_ref, vg_ref, qm_ref, km_ref, vm_ref, pool_ref):
    tm = x_ref.shape[2]
    wide = functools.partial(_wide, width=tm)
    x = x_ref[0]
    d_model = x.shape[0]
    xn = x * _rms_rows(x, d_model) * wide(gn_ref[0])
    h = xn * (1.0 + wide(sc_ref[0, 0, 0])) + wide(sh_ref[0, 0, 0])
    hb = h.astype(BF16)
    gp = gp_ref[0]

    def gain(off, n):
        return wide(gp[off:off + n])

    u_lat = _dot(w_ref[0, :_N_LAT, :], hb)
    u = _dot(w_ref[0, _N_LAT:, :], hb)
    cq = u_lat[_L_CQ:_L_CQ + MLA_Q_RANK]
    cq_n = cq * _rms_rows(cq, MLA_Q_RANK) * gain(_G_CQ, MLA_Q_RANK)
    qm_all = _dot(wuq_ref[0], cq_n.astype(BF16))
    ckv = u_lat[_L_CKV:_L_CKV + MLA_KV_RANK]
    ckv_n = ckv * _rms_rows(ckv, MLA_KV_RANK) * gain(_G_CKV, MLA_KV_RANK)
    kv_all = _dot(wukv_ref[0], ckv_n.astype(BF16))
    kr = u_lat[_L_KR:_L_KR + MLA_ROPE]
    kr_sw = u_lat[_L_KRS:_L_KRS + MLA_ROPE]
    kr_ss = jnp.sum(kr * kr, axis=0, keepdims=True)

    cg = rope_ref[_R_CG:_R_CG + HEAD_DIM]
    sg = rope_ref[_R_SG:_R_SG + HEAD_DIM]
    cm = rope_ref[_R_CM:_R_CM + MLA_ROPE]
    sm = rope_ref[_R_SM:_R_SM + MLA_ROPE]
    zeros_kv = jnp.zeros((HEAD_DIM, tm), F32)
    zeros_pad = jnp.zeros((QK_PAD - MLA_QK, tm), F32)

    scale_g = HEAD_DIM ** -0.5 * LOG2E
    gcq = gain(_G_Q, HEAD_DIM) * cg * scale_g
    gsq = gain(_G_QS, HEAD_DIM) * sg * scale_g
    for hd in range(GQA_HEADS):
        a = u[_O_Q + hd * HEAD_DIM:_O_Q + (hd + 1) * HEAD_DIM]
        a_sw = u[_O_QS + hd * HEAD_DIM:_O_QS + (hd + 1) * HEAD_DIM]
        q = _rms_rows(a, HEAD_DIM) * (a * gcq + a_sw * gsq)
        parts = [zeros_kv] * GQA_KV_HEADS
        parts[hd // GQA_GROUP] = q
        qg_ref[0, hd] = jnp.concatenate(parts, axis=0).astype(BF16)
    gck = gain(_G_K, HEAD_DIM) * cg
    gsk = gain(_G_KS, HEAD_DIM) * sg
    ks = []
    for hd in range(GQA_KV_HEADS):
        a = u[_O_K + hd * HEAD_DIM:_O_K + (hd + 1) * HEAD_DIM]
        a_sw = u[_O_KS + hd * HEAD_DIM:_O_KS + (hd + 1) * HEAD_DIM]
        ks.append(_rms_rows(a, HEAD_DIM) * (a * gck + a_sw * gsk))
    kg_ref[0, 0] = jnp.concatenate(ks, axis=0).T.astype(BF16)
    for hd in range(GQA_KV_HEADS):
        vg_ref[0, hd] = u[_O_V + hd * HEAD_DIM:_O_V + (hd + 1) * HEAD_DIM].astype(BF16)

    scale_m = MLA_QK ** -0.5 * LOG2E
    gq_n = gain(_G_QM_N, MLA_NOPE) * scale_m
    gq_c = gain(_G_QM_R, MLA_ROPE) * cm * scale_m
    gq_s = gain(_G_QM_RS, MLA_ROPE) * sm * scale_m
    gk_n = gain(_G_KM_N, MLA_NOPE)
    gk_c = gain(_G_KM_R, MLA_ROPE) * cm
    gk_s = gain(_G_KM_RS, MLA_ROPE) * sm
    k_rope = kr * gk_c + kr_sw * gk_s
    sw_base = MLA_HEADS * MLA_QK
    for hd in range(MLA_HEADS):
        a = qm_all[hd * MLA_QK:(hd + 1) * MLA_QK]
        a_sw = qm_all[sw_base + hd * MLA_ROPE:sw_base + (hd + 1) * MLA_ROPE]
        r = _rms_rows(a, MLA_QK)
        a_n, a_r = a[:MLA_NOPE], a[MLA_NOPE:]
        q = jnp.concatenate([r * (a_n * gq_n), r * (a_r * gq_c + a_sw * gq_s), zeros_pad], axis=0)
        qm_ref[0, hd] = q.astype(BF16)

        kvh = kv_all[hd * (MLA_NOPE + MLA_V):(hd + 1) * (MLA_NOPE + MLA_V)]
        kn, v = kvh[:MLA_NOPE], kvh[MLA_NOPE:]
        rk = lax.rsqrt((jnp.sum(kn * kn, axis=0, keepdims=True) + kr_ss) * (1.0 / MLA_QK) + NORM_EPS)
        k = jnp.concatenate([rk * (kn * gk_n), rk * k_rope, zeros_pad], axis=0)
        km_ref[0, hd] = k.T.astype(BF16)
        vm_ref[0, hd] = v.astype(BF16)

    pool_ref[0] = u[_O_POOL:_O_POOL + POOL_WIDTH]


def _proj_call(layer, xT, modT, gn1, w_inT, gpack, ropeT, w_uqT, w_ukvT, n_ctx_tiles):
    b, d, t = xT.shape
    tm = TOKEN_TILE
    nt = t // tm
    nb = modT.shape[1] - 1

    def mod_spec(slot):
        return pl.BlockSpec((1, 1, 1, d, LANES),
                            lambda bi, ti: (layer, jnp.where(ti < n_ctx_tiles, nb, bi), slot, 0, 0))

    def const_spec(arr):
        return pl.BlockSpec((1,) + arr.shape[1:], lambda bi, ti: (layer,) + (0,) * (arr.ndim - 1))

    out_shapes = (
        jax.ShapeDtypeStruct((b, GQA_HEADS, QK_PAD, t), BF16),
        jax.ShapeDtypeStruct((b, 1, t, QK_PAD), BF16),
        jax.ShapeDtypeStruct((b, GQA_KV_HEADS, V_ROWS, t), BF16),
        jax.ShapeDtypeStruct((b, MLA_HEADS, QK_PAD, t), BF16),
        jax.ShapeDtypeStruct((b, MLA_HEADS, t, QK_PAD), BF16),
        jax.ShapeDtypeStruct((b, MLA_HEADS, V_ROWS, t), BF16),
        jax.ShapeDtypeStruct((b, POOL_WIDTH, t), F32),
    )
    out_specs = (
        pl.BlockSpec((1, GQA_HEADS, QK_PAD, tm), lambda bi, ti: (bi, 0, 0, ti)),
        pl.BlockSpec((1, 1, tm, QK_PAD), lambda bi, ti: (bi, 0, ti, 0)),
        pl.BlockSpec((1, GQA_KV_HEADS, V_ROWS, tm), lambda bi, ti: (bi, 0, 0, ti)),
        pl.BlockSpec((1, MLA_HEADS, QK_PAD, tm), lambda bi, ti: (bi, 0, 0, ti)),
        pl.BlockSpec((1, MLA_HEADS, tm, QK_PAD), lambda bi, ti: (bi, 0, ti, 0)),
        pl.BlockSpec((1, MLA_HEADS, V_ROWS, tm), lambda bi, ti: (bi, 0, 0, ti)),
        pl.BlockSpec((1, POOL_WIDTH, tm), lambda bi, ti: (bi, 0, ti)),
    )
    return pl.pallas_call(
        _proj_kernel,
        grid=(b, nt),
        in_specs=[
            pl.BlockSpec((1, d, tm), lambda bi, ti: (bi, 0, ti)),
            mod_spec(0), mod_spec(1),
            const_spec(gn1), const_spec(w_inT), const_spec(gpack),
            pl.BlockSpec((_R_ROWS, tm), lambda bi, ti: (0, ti)),
            const_spec(w_uqT), const_spec(w_ukvT),
        ],
        out_specs=out_specs,
        out_shape=out_shapes,
        compiler_params=pltpu.CompilerParams(
            dimension_semantics=("arbitrary", "arbitrary"), vmem_limit_bytes=VMEM_LIMIT),
        name="proj",
    )(xT, modT, modT, gn1, w_inT, gpack, ropeT, w_uqT, w_ukvT)


def _attn_kernel(flag_ref, q_ref, k_ref, v_ref, o_ref, acc_ref, l_ref, *, shared_kv, n_ctx_tiles,
                 ctx_len, seq_len, kv_chunk):
    heads = q_ref.shape[1]
    tq = q_ref.shape[3]
    is_ctx = pl.program_id(2) < n_ctx_tiles

    def kv(g):
        return 0 if shared_kv else g

    bounded = flag_ref[0] != 0

    def block(start, size, first):
        started = set()

        def finish_chain(g, lo, s):
            p = jnp.exp2(s)
            p_sum = jnp.sum(p.reshape(KV_SUB // SUBLANES, SUBLANES, tq), axis=0)
            pv = _dot(v_ref[0, kv(g), :, pl.ds(lo, KV_SUB)], p.astype(BF16))
            if first and g not in started:
                acc_ref[g] = pv
                l_ref[g] = p_sum
            else:
                acc_ref[g] += pv
                l_ref[g] += p_sum
            started.add(g)

        pending = []
        for c in range(size // KV_SUB):
            lo = start + c * KV_SUB
            for g in range(heads):
                s = _dot(k_ref[0, kv(g), pl.ds(lo, KV_SUB), :], q_ref[0, g])
                pending.append((g, lo, s))
                if len(pending) > CHAIN_SKEW:
                    finish_chain(*pending.pop(0))
        for chain in pending:
            finish_chain(*chain)

    @pl.when(jnp.logical_and(bounded, is_ctx))
    def _bounded_scores_ctx():
        block(0, ctx_len, True)

    @pl.when(jnp.logical_and(bounded, jnp.logical_not(is_ctx)))
    def _bounded_scores_latent():
        block(0, kv_chunk, True)

        def step(j, carry):
            block(pl.multiple_of(j * kv_chunk, kv_chunk), kv_chunk, False)
            return carry

        lax.fori_loop(1, (ctx_len + seq_len) // kv_chunk, step, 0)

    @pl.when(jnp.logical_not(bounded))
    def _online_softmax():
        chunk = KV_CHUNK_ONLINE
        n = jnp.where(is_ctx, ctx_len // chunk, (ctx_len + seq_len) // chunk)
        for g in range(heads):
            q = q_ref[0, g]

            def step(j, carry, g=g, q=q):
                m, l, acc = carry
                start = pl.multiple_of(j * chunk, chunk)
                s = _dot(k_ref[0, kv(g), pl.ds(start, chunk), :], q)
                m_new = jnp.maximum(m, jnp.max(s, axis=0, keepdims=True))
                alpha = jnp.exp2(m - m_new)
                p = jnp.exp2(s - m_new)
                pv = _dot(v_ref[0, kv(g), :, pl.ds(start, chunk)], p.astype(BF16))
                return m_new, alpha * l + jnp.sum(p, axis=0, keepdims=True), alpha * acc + pv

            init = (jnp.full((1, tq), -jnp.inf, F32), jnp.zeros((1, tq), F32),
                    jnp.zeros((V_ROWS, tq), F32))
            _, l, acc = lax.fori_loop(0, n, step, init)
            acc_ref[g] = acc
            l_ref[g] = jnp.concatenate([l, jnp.zeros((SUBLANES - 1, tq), F32)], axis=0)

    for g in range(heads):
        l = jnp.sum(l_ref[g], axis=0, keepdims=True)
        o_ref[0, g * HEAD_DIM:(g + 1) * HEAD_DIM] = (acc_ref[g] / l).astype(BF16)


def _kv_chunk(t):
    return max(c for c in range(Q_TILE, min(t, KV_CHUNK_MAX) + 1, Q_TILE) if t % c == 0)


def _attn_call(flag, qT, k, vT, shared_kv, n_ctx_tiles, ctx_len, name):
    b, nh, _, t = qT.shape
    tq = Q_TILE
    g = HEADS_PER_STEP
    nkv = 1 if shared_kv else g
    kern = functools.partial(_attn_kernel, shared_kv=shared_kv, n_ctx_tiles=n_ctx_tiles,
                             ctx_len=ctx_len, seq_len=t - ctx_len, kv_chunk=_kv_chunk(t))
    k_map = ((lambda bi, gi, qi, f: (bi, 0, 0, 0)) if shared_kv
             else (lambda bi, gi, qi, f: (bi, gi, 0, 0)))
    grid_spec = pltpu.PrefetchScalarGridSpec(
        num_scalar_prefetch=1,
        grid=(b, nh // g, t // tq),
        in_specs=[
            pl.BlockSpec((1, g, QK_PAD, tq), lambda bi, gi, qi, f: (bi, gi, 0, qi)),
            pl.BlockSpec((1, nkv, t, QK_PAD), k_map),
            pl.BlockSpec((1, nkv, V_ROWS, t), lambda bi, gi, qi, f: (bi, gi, 0, 0)),
        ],
        out_specs=pl.BlockSpec((1, g * HEAD_DIM, tq), lambda bi, gi, qi, f: (bi, gi, qi)),
        scratch_shapes=[pltpu.VMEM((g, V_ROWS, tq), F32), pltpu.VMEM((g, SUBLANES, tq), F32)],
    )
    return pl.pallas_call(
        kern,
        grid_spec=grid_spec,
        out_shape=jax.ShapeDtypeStruct((b, nh * HEAD_DIM, t), BF16),
        compiler_params=pltpu.CompilerParams(
            dimension_semantics=("arbitrary", "arbitrary", "arbitrary"),
            vmem_limit_bytes=VMEM_LIMIT),
        name=name,
    )(flag, qT, k, vT)


def _mix_mlp_kernel(x_ref, og_ref, om_ref, pp_ref, pc_ref, pn_ref, gt1_ref, sh2_ref, sc2_ref,
                    gt2_ref, gn_ref, ls_ref, wp_ref, wo_ref, w1_ref, w2_ref, o_ref,
                    *, n_ctx_tiles, n_tiles):
    tm = x_ref.shape[2]
    wide = functools.partial(_wide, width=tm)
    ti = pl.program_id(1)

    cur = pc_ref[0]
    prev_ok = jnp.logical_and(ti != 0, ti != n_ctx_tiles)
    next_ok = jnp.logical_and(ti != n_ctx_tiles - 1, ti != n_tiles - 1)
    prev = jnp.where(prev_ok, pp_ref[0][:, tm - LANES:], 0.0)
    nxt = jnp.where(next_ok, pn_ref[0][:, :LANES], 0.0)
    ext = jnp.concatenate([prev, cur, nxt], axis=1)
    width = tm + 2 * LANES

    def shifted(a, s):
        return pltpu.roll(a, s % width, axis=1)

    in_ctx = ti < n_ctx_tiles
    seg_start = jnp.where(in_ctx, 0, n_ctx_tiles * tm)
    seg_len = jnp.where(in_ctx, n_ctx_tiles * tm, (n_tiles - n_ctx_tiles) * tm)
    pos = lax.broadcasted_iota(jnp.int32, (1, tm), 1) + (ti * tm - seg_start)
    gd = POOL_GROUP_DIM
    run = ext + shifted(ext, 1)
    outs = []
    for gi, w in enumerate(POOL_WINDOWS):
        if gi > 0:
            quarter = w // 4
            run = run[gd:]
            run = shifted(run, quarter) + shifted(run, -quarter)
        cnt = jnp.minimum(pos + w // 2, seg_len) - jnp.maximum(pos - w // 2, 0)
        mean = run[:gd, LANES:LANES + tm] / cnt.astype(F32)
        dlt = (mean - cur[gi * gd:(gi + 1) * gd]).astype(BF16)
        outs.append(_dot(wp_ref[0, gi], dlt))
    op = jnp.concatenate(outs, axis=0) * wide(ls_ref[0])

    mix = jnp.concatenate([og_ref[0], om_ref[0], op.astype(BF16)], axis=0)
    half = wo_ref.shape[1] // 2
    y = jnp.concatenate([_dot(wo_ref[0, :half, :], mix), _dot(wo_ref[0, half:, :], mix)], axis=0)
    x1 = x_ref[0] + wide(gt1_ref[0, 0, 0]) * y

    d_model = x1.shape[0]
    xn = x1 * _rms_rows(x1, d_model) * wide(gn_ref[0])
    h2 = (xn * (1.0 + wide(sc2_ref[0, 0, 0])) + wide(sh2_ref[0, 0, 0])).astype(BF16)
    d_ff = w1_ref.shape[1]
    n_chunks = d_ff // FF_CHUNK

    def up(c):
        a = jnp.maximum(_dot(w1_ref[0, c * FF_CHUNK:(c + 1) * FF_CHUNK, :], h2), 0.0)
        return (a * a).astype(BF16)

    y2 = None
    hidden = up(0)
    for c in range(n_chunks):
        nxt = up(c + 1) if c + 1 < n_chunks else None
        part = _dot(w2_ref[0, :, c * FF_CHUNK:(c + 1) * FF_CHUNK], hidden)
        y2 = part if y2 is None else y2 + part
        hidden = nxt
    o_ref[0] = x1 + wide(gt2_ref[0, 0, 0]) * y2


def _mix_mlp_call(layer, xT, ogT, omT, poolT, modT, gn2, ls_rep, w_poolT, w_outT, w1T, w2T,
                  n_ctx_tiles):
    b, d, t = xT.shape
    tm = TOKEN_TILE
    nt = t // tm
    nb = modT.shape[1] - 1

    def mod_spec(slot):
        return pl.BlockSpec((1, 1, 1, d, LANES),
                            lambda bi, ti: (layer, jnp.where(ti < n_ctx_tiles, nb, bi), slot, 0, 0))

    def const_spec(arr):
        return pl.BlockSpec((1,) + arr.shape[1:], lambda bi, ti: (layer,) + (0,) * (arr.ndim - 1))

    def tok_spec(rows):
        return pl.BlockSpec((1, rows, tm), lambda bi, ti: (bi, 0, ti))

    kern = functools.partial(_mix_mlp_kernel, n_ctx_tiles=n_ctx_tiles, n_tiles=nt)
    return pl.pallas_call(
        kern,
        grid=(b, nt),
        in_specs=[
            tok_spec(d), tok_spec(ogT.shape[1]), tok_spec(omT.shape[1]),
            pl.BlockSpec((1, POOL_WIDTH, tm), lambda bi, ti: (bi, 0, jnp.maximum(ti - 1, 0))),
            tok_spec(POOL_WIDTH),
            pl.BlockSpec((1, POOL_WIDTH, tm), lambda bi, ti: (bi, 0, jnp.minimum(ti + 1, nt - 1))),
            mod_spec(2), mod_spec(3), mod_spec(4), mod_spec(5),
            const_spec(gn2), const_spec(ls_rep), const_spec(w_poolT), const_spec(w_outT),
            const_spec(w1T), const_spec(w2T),
        ],
        out_specs=tok_spec(d),
        out_shape=jax.ShapeDtypeStruct((b, d, t), F32),
        compiler_params=pltpu.CompilerParams(
            dimension_semantics=("arbitrary", "arbitrary"), vmem_limit_bytes=VMEM_LIMIT),
        name="mix_mlp",
    )(xT, ogT, omT, poolT, poolT, poolT, modT, modT, modT, modT, gn2, ls_rep, w_poolT, w_outT,
      w1T, w2T)


def _rope_table(seq_len, ctx_len):
    rows = seq_len // GRID_W
    row = jnp.repeat(jnp.arange(rows, dtype=F32), GRID_W)
    col = jnp.tile(jnp.arange(GRID_W, dtype=F32), rows)

    def block(rot_dim):
        n = rot_dim // 4
        inv = ROPE_THETA ** (-jnp.arange(n, dtype=F32) / n)
        ang = jnp.concatenate([row[:, None] * inv, col[:, None] * inv], axis=-1)
        cos, sin = jnp.cos(ang).T, jnp.sin(ang).T
        c = jnp.concatenate([cos, cos], axis=0)
        s = jnp.concatenate([-sin, sin], axis=0)
        c = jnp.concatenate([jnp.ones((rot_dim, ctx_len), F32), c], axis=1)
        s = jnp.concatenate([jnp.zeros((rot_dim, ctx_len), F32), s], axis=1)
        return c, s

    cg, sg = block(HEAD_DIM)
    cm, sm = block(MLA_ROPE)
    return jnp.concatenate([cg, sg, cm, sm], axis=0)


def _lane_rep(a):
    return jnp.broadcast_to(a[..., None], a.shape + (LANES,)).astype(F32)


def _swap_half(n):
    return (np.arange(n) + n // 2) % n


def _prep_weights(w_mod, b_mod, g_norm1, g_norm2, w_in, g_q_gqa, g_k_gqa, g_cq, g_ckv, w_uq, w_ukv,
                  g_q_mla, g_k_mla, w_pool, ls_pool, w_out, w_mlp1, w_mlp2):
    nq, nk = GQA_HEADS * HEAD_DIM, GQA_KV_HEADS * HEAD_DIM
    o_q, o_k, o_v = 0, nq, nq + nk
    o_cq = o_v + nk
    o_ckv = o_cq + MLA_Q_RANK
    o_kr = o_ckv + MLA_KV_RANK
    o_pool = o_kr + MLA_ROPE
    sw64 = _swap_half(HEAD_DIM)
    sw32 = _swap_half(MLA_ROPE)
    q_sw = np.concatenate([o_q + h * HEAD_DIM + sw64 for h in range(GQA_HEADS)])
    k_sw = np.concatenate([o_k + h * HEAD_DIM + sw64 for h in range(GQA_KV_HEADS)])
    cols = np.concatenate([
        np.arange(o_cq, o_cq + MLA_Q_RANK), np.arange(o_ckv, o_ckv + MLA_KV_RANK),
        np.arange(o_kr, o_kr + MLA_ROPE), o_kr + sw32,
        np.arange(o_q, o_q + nq), q_sw, np.arange(o_k, o_k + nk), k_sw, np.arange(o_v, o_v + nk),
        np.arange(o_pool, o_pool + POOL_WIDTH)])
    assert cols.shape[0] == _IN_EXT
    w_inT = jnp.transpose(w_in[:, :, cols], (0, 2, 1)).astype(BF16)

    uq_sw = np.concatenate([h * MLA_QK + MLA_NOPE + sw32 for h in range(MLA_HEADS)])
    uq_cols = np.concatenate([np.arange(MLA_HEADS * MLA_QK), uq_sw])
    w_uqT = jnp.transpose(w_uq[:, :, uq_cols], (0, 2, 1)).astype(BF16)
    w_ukvT = jnp.transpose(w_ukv, (0, 2, 1)).astype(BF16)

    gq_r, gk_r = g_q_mla[:, MLA_NOPE:], g_k_mla[:, MLA_NOPE:]
    gpack = jnp.concatenate([
        g_q_gqa, g_q_gqa[:, sw64], g_k_gqa, g_k_gqa[:, sw64], g_cq, g_ckv,
        g_q_mla[:, :MLA_NOPE], gq_r, gq_r[:, sw32], g_k_mla[:, :MLA_NOPE], gk_r, gk_r[:, sw32]], axis=1)
    assert gpack.shape[1] == _G_ROWS
    return dict(
        w_modT=jnp.transpose(w_mod, (0, 2, 1)).astype(BF16),
        b_rep=_lane_rep(b_mod),
        gn1=_lane_rep(g_norm1), gn2=_lane_rep(g_norm2),
        w_inT=w_inT, w_uqT=w_uqT, w_ukvT=w_ukvT, gpack=_lane_rep(gpack),
        w_poolT=jnp.transpose(w_pool, (0, 1, 3, 2)).astype(BF16),
        ls_rep=_lane_rep(ls_pool),
        w_outT=jnp.transpose(w_out, (0, 2, 1)).astype(BF16),
        w1T=jnp.transpose(w_mlp1, (0, 2, 1)).astype(BF16),
        w2T=jnp.transpose(w_mlp2, (0, 2, 1)).astype(BF16),
    )


def kernel(x, c, ctx, c_ctx, w_mod, b_mod, g_norm1, g_norm2, w_in, g_q_gqa, g_k_gqa, g_cq, g_ckv, w_uq, w_ukv, g_q_mla, g_k_mla, w_pool, ls_pool, w_out, w_mlp1, w_mlp2):
    batch, seq, d_model = x.shape
    ctx_len = ctx.shape[1]
    depth = w_in.shape[0]
    assert ctx_len % TOKEN_TILE == 0 and seq % TOKEN_TILE == 0 and seq % GRID_W == 0
    assert TOKEN_TILE == Q_TILE
    assert ctx_len % KV_CHUNK_ONLINE == 0 and seq % KV_CHUNK_ONLINE == 0
    assert GQA_GROUP == HEADS_PER_STEP and MLA_HEADS % HEADS_PER_STEP == 0
    n_ctx_tiles = ctx_len // TOKEN_TILE

    def bounded_flag(g_q, g_k, dim):
        bound = dim ** 0.5 * LOG2E * jnp.max(jnp.abs(g_q), axis=1) * jnp.max(jnp.abs(g_k), axis=1)
        return (bound <= SAFE_LOG2_BOUND).astype(jnp.int32)

    flag_g = bounded_flag(g_q_gqa, g_k_gqa, HEAD_DIM)
    flag_m = bounded_flag(g_q_mla, g_k_mla, MLA_QK)

    p = _prep_weights(w_mod, b_mod, g_norm1, g_norm2, w_in, g_q_gqa, g_k_gqa, g_cq, g_ckv, w_uq,
                      w_ukv, g_q_mla, g_k_mla, w_pool, ls_pool, w_out, w_mlp1, w_mlp2)
    ropeT = _rope_table(seq, ctx_len)
    c_rep = _lane_rep(jnp.concatenate([c, c_ctx[None, :]], axis=0))
    modT = _mod_call(p["w_modT"], c_rep, p["b_rep"])

    xT = jnp.transpose(jnp.concatenate([ctx, x], axis=1), (0, 2, 1))
    for i in range(depth):
        qg, kg, vg, qm, km, vm, pool_in = _proj_call(
            i, xT, modT, p["gn1"], p["w_inT"], p["gpack"], ropeT, p["w_uqT"], p["w_ukvT"], n_ctx_tiles)
        ogT = _attn_call(flag_g[i:i + 1], qg, kg, vg, True, n_ctx_tiles, ctx_len, "attn_gqa")
        omT = _attn_call(flag_m[i:i + 1], qm, km, vm, False, n_ctx_tiles, ctx_len, "attn_mla")
        xT = _mix_mlp_call(i, xT, ogT, omT, pool_in, modT, p["gn2"], p["ls_rep"], p["w_poolT"],
                           p["w_outT"], p["w1T"], p["w2T"], n_ctx_tiles)
    return jnp.transpose(xT[:, :, ctx_len:], (0, 2, 1))
```

```python
import functools
import math

import numpy as np
import jax
import jax.numpy as jnp
from jax import lax
from jax.experimental import pallas as pl
from jax.experimental.pallas import tpu as pltpu

F32 = jnp.float32
BF16 = jnp.bfloat16

GRID_W = 64
ROPE_THETA = 10000.0
NORM_EPS = 1e-6
GQA_HEADS = 6
GQA_KV_HEADS = 2
GQA_GROUP = GQA_HEADS // GQA_KV_HEADS
HEAD_DIM = 64
MLA_HEADS = 6
MLA_NOPE = 64
MLA_ROPE = 32
MLA_QK = MLA_NOPE + MLA_ROPE
MLA_V = 64
MLA_Q_RANK = 256
MLA_KV_RANK = 128
POOL_WINDOWS = (2, 4, 8, 16)
POOL_GROUP_DIM = 64
POOL_WIDTH = len(POOL_WINDOWS) * POOL_GROUP_DIM
N_MOD = 6

LANES = 128
QK_PAD = 128
TOKEN_TILE = 256
Q_TILE = 256
KV_CHUNK_MAX = 8448
KV_SUB = 256
CHAIN_SKEW = 5
KV_CHUNK_ONLINE = 256
HEADS_PER_STEP = 3
V_ROWS = HEAD_DIM
SUBLANES = 8
FF_CHUNK = 256
VMEM_LIMIT = 56 * 1024 * 1024
LOG2E = 1.4426950408889634
SAFE_LOG2_BOUND = 64.0

_L_CQ = 0
_L_CKV = _L_CQ + MLA_Q_RANK
_L_KR = _L_CKV + MLA_KV_RANK
_N_LAT = _L_KR + MLA_ROPE
_O_Q = 0
_O_K = _O_Q + GQA_HEADS * HEAD_DIM
_O_V = _O_K + GQA_KV_HEADS * HEAD_DIM
_O_POOL = _O_V + GQA_KV_HEADS * HEAD_DIM
_IN_EXT = _N_LAT + _O_POOL + POOL_WIDTH

_G_Q = 0
_G_QS = _G_Q + HEAD_DIM
_G_K = _G_QS + HEAD_DIM
_G_KS = _G_K + HEAD_DIM
_G_CQ = _G_KS + HEAD_DIM
_G_CKV = _G_CQ + MLA_Q_RANK
_G_QM_N = _G_CKV + MLA_KV_RANK
_G_QM_R = _G_QM_N + MLA_NOPE
_G_QM_RS = _G_QM_R + MLA_ROPE
_G_KM_N = _G_QM_RS + MLA_ROPE
_G_KM_R = _G_KM_N + MLA_NOPE
_G_KM_RS = _G_KM_R + MLA_ROPE
_G_ROWS = _G_KM_RS + MLA_ROPE

_R_CG = 0
_R_SG = _R_CG + HEAD_DIM
_R_CM = _R_SG + HEAD_DIM
_R_SM = _R_CM + MLA_ROPE
_R_ROWS = _R_SM + MLA_ROPE


def _wide(a, width):
    return jnp.concatenate([a] * (width // LANES), axis=1)


def _dot(a, b):
    return jnp.dot(a, b, preferred_element_type=F32)


def _to_feature_major_kernel(ctx_ref, x_ref, o_ref, *, n_ctx_tiles):
    is_ctx = pl.program_id(1) < n_ctx_tiles

    @pl.when(is_ctx)
    def _():
        o_ref[0] = ctx_ref[0].T

    @pl.when(jnp.logical_not(is_ctx))
    def _():
        o_ref[0] = x_ref[0].T


def _to_feature_major_call(ctx, x, n_ctx_tiles):
    b, seq, d = x.shape
    tm = TOKEN_TILE
    nt = n_ctx_tiles + seq // tm
    return pl.pallas_call(
        functools.partial(_to_feature_major_kernel, n_ctx_tiles=n_ctx_tiles),
        grid=(b, nt),
        in_specs=[
            pl.BlockSpec((1, tm, d), lambda bi, ti: (bi, jnp.minimum(ti, n_ctx_tiles - 1), 0)),
            pl.BlockSpec((1, tm, d), lambda bi, ti: (bi, jnp.maximum(ti - n_ctx_tiles, 0), 0)),
        ],
        out_specs=pl.BlockSpec((1, d, tm), lambda bi, ti: (bi, 0, ti)),
        out_shape=jax.ShapeDtypeStruct((b, d, nt * tm), F32),
        compiler_params=pltpu.CompilerParams(
            dimension_semantics=("arbitrary", "arbitrary"), vmem_limit_bytes=VMEM_LIMIT),
        name="to_feature_major",
    )(ctx, x)


def _mod_kernel(w_ref, c_ref, b_ref, o_ref):
    w = w_ref[0]
    b = b_ref[0]
    for j in range(c_ref.shape[0]):
        c = c_ref[j]
        s = c / (1.0 + jnp.exp(-c))
        o_ref[0, j, 0] = _dot(w, s.astype(BF16)) + b


def _mod_call(w_modT, c_rep, b_rep):
    depth, six_d, d = w_modT.shape
    nvec = c_rep.shape[0]
    return pl.pallas_call(
        _mod_kernel,
        grid=(depth, N_MOD),
        in_specs=[
            pl.BlockSpec((1, d, d), lambda i, n: (i, n, 0)),
            pl.BlockSpec((nvec, d, LANES), lambda i, n: (0, 0, 0)),
            pl.BlockSpec((1, d, LANES), lambda i, n: (i, n, 0)),
        ],
        out_specs=pl.BlockSpec((1, nvec, 1, d, LANES), lambda i, n: (i, 0, n, 0, 0)),
        out_shape=jax.ShapeDtypeStruct((depth, nvec, N_MOD, d, LANES), F32),
        compiler_params=pltpu.CompilerParams(
            dimension_semantics=("arbitrary", "arbitrary"), vmem_limit_bytes=VMEM_LIMIT),
        name="mod_vectors",
    )(w_modT, c_rep, b_rep)


def _swap_halves(a):
    half = a.shape[0] // 2
    return jnp.concatenate([a[half:], a[:half]], axis=0)


def _rms_rows(a, n):
    return lax.rsqrt(jnp.sum(a * a, axis=0, keepdims=True) * (1.0 / n) + NORM_EPS)


def _proj_kernel(x_ref, sh_ref, sc_ref, gn_ref, w_ref, gp_ref, rope_ref, wuq_ref, wukv_ref,
                 qg_ref, kg_ref, vg_ref, qm_ref, km_ref, vm_ref, pool_ref):
    tm = x_ref.shape[2]
    wide = functools.partial(_wide, width=tm)
    x = x_ref[0]
    d_model = x.shape[0]
    xn = x * _rms_rows(x, d_model) * wide(gn_ref[0])
    h = xn * (1.0 + wide(sc_ref[0, 0, 0])) + wide(sh_ref[0, 0, 0])
    hb = h.astype(BF16)
    gp = gp_ref[0]

    def gain(off, n):
        return wide(gp[off:off + n])

    u_lat = _dot(w_ref[0, :_N_LAT, :], hb)
    u = _dot(w_ref[0, _N_LAT:, :], hb)
    cq = u_lat[_L_CQ:_L_CQ + MLA_Q_RANK]
    cq_n = cq * _rms_rows(cq, MLA_Q_RANK) * gain(_G_CQ, MLA_Q_RANK)
    qm_all = _dot(wuq_ref[0], cq_n.astype(BF16))
    ckv = u_lat[_L_CKV:_L_CKV + MLA_KV_RANK]
    ckv_n = ckv * _rms_rows(ckv, MLA_KV_RANK) * gain(_G_CKV, MLA_KV_RANK)
    kv_all = _dot(wukv_ref[0], ckv_n.astype(BF16))
    kr = u_lat[_L_KR:_L_KR + MLA_ROPE]
    kr_sw = _swap_halves(kr)
    kr_ss = jnp.sum(kr * kr, axis=0, keepdims=True)

    cg = rope_ref[_R_CG:_R_CG + HEAD_DIM]
    sg = rope_ref[_R_SG:_R_SG + HEAD_DIM]
    cm = rope_ref[_R_CM:_R_CM + MLA_ROPE]
    sm = rope_ref[_R_SM:_R_SM + MLA_ROPE]
    zeros_kv = jnp.zeros((HEAD_DIM, tm), F32)
    zeros_pad = jnp.zeros((QK_PAD - MLA_QK, tm), F32)

    scale_g = HEAD_DIM ** -0.5 * LOG2E
    gcq = gain(_G_Q, HEAD_DIM) * cg * scale_g
    gsq = gain(_G_QS, HEAD_DIM) * sg * scale_g
    for hd in range(GQA_HEADS):
        a = u[_O_Q + hd * HEAD_DIM:_O_Q + (hd + 1) * HEAD_DIM]
        q = _rms_rows(a, HEAD_DIM) * (a * gcq + _swap_halves(a) * gsq)
        parts = [zeros_kv] * GQA_KV_HEADS
        parts[hd // GQA_GROUP] = q
        qg_ref[0, hd] = jnp.concatenate(parts, axis=0).astype(BF16)
    gck = gain(_G_K, HEAD_DIM) * cg
    gsk = gain(_G_KS, HEAD_DIM) * sg
    ks = []
    for hd in range(GQA_KV_HEADS):
        a = u[_O_K + hd * HEAD_DIM:_O_K + (hd + 1) * HEAD_DIM]
        ks.append(_rms_rows(a, HEAD_DIM) * (a * gck + _swap_halves(a) * gsk))
    kg_ref[0, 0] = jnp.concatenate(ks, axis=0).T.astype(BF16)
    for hd in range(GQA_KV_HEADS):
        vg_ref[0, hd] = u[_O_V + hd * HEAD_DIM:_O_V + (hd + 1) * HEAD_DIM].astype(BF16)

    scale_m = MLA_QK ** -0.5 * LOG2E
    gq_n = gain(_G_QM_N, MLA_NOPE) * scale_m
    gq_c = gain(_G_QM_R, MLA_ROPE) * cm * scale_m
    gq_s = gain(_G_QM_RS, MLA_ROPE) * sm * scale_m
    gk_n = gain(_G_KM_N, MLA_NOPE)
    gk_c = gain(_G_KM_R, MLA_ROPE) * cm
    gk_s = gain(_G_KM_RS, MLA_ROPE) * sm
    k_rope = kr * gk_c + kr_sw * gk_s
    for hd in range(MLA_HEADS):
        a = qm_all[hd * MLA_QK:(hd + 1) * MLA_QK]
        r = _rms_rows(a, MLA_QK)
        a_n, a_r = a[:MLA_NOPE], a[MLA_NOPE:]
        q = jnp.concatenate(
            [r * (a_n * gq_n), r * (a_r * gq_c + _swap_halves(a_r) * gq_s), zeros_pad], axis=0)
        qm_ref[0, hd] = q.astype(BF16)

        kvh = kv_all[hd * (MLA_NOPE + MLA_V):(hd + 1) * (MLA_NOPE + MLA_V)]
        kn, v = kvh[:MLA_NOPE], kvh[MLA_NOPE:]
        rk = lax.rsqrt((jnp.sum(kn * kn, axis=0, keepdims=True) + kr_ss) * (1.0 / MLA_QK) + NORM_EPS)
        k = jnp.concatenate([rk * (kn * gk_n), rk * k_rope, zeros_pad], axis=0)
        km_ref[0, hd] = k.T.astype(BF16)
        vm_ref[0, hd] = v.astype(BF16)

    pool_ref[0] = u[_O_POOL:_O_POOL + POOL_WIDTH]


def _proj_call(layer, xT, modT, gn1, w_inT, gpack, ropeT, w_uqT, w_ukvT, n_ctx_tiles):
    b, d, t = xT.shape
    tm = TOKEN_TILE
    nt = t // tm
    nb = modT.shape[1] - 1

    def mod_spec(slot):
        return pl.BlockSpec((1, 1, 1, d, LANES),
                            lambda bi, ti: (layer, jnp.where(ti < n_ctx_tiles, nb, bi), slot, 0, 0))

    def const_spec(arr):
        return pl.BlockSpec((1,) + arr.shape[1:], lambda bi, ti: (layer,) + (0,) * (arr.ndim - 1))

    out_shapes = (
        jax.ShapeDtypeStruct((b, GQA_HEADS, QK_PAD, t), BF16),
        jax.ShapeDtypeStruct((b, 1, t, QK_PAD), BF16),
        jax.ShapeDtypeStruct((b, GQA_KV_HEADS, V_ROWS, t), BF16),
        jax.ShapeDtypeStruct((b, MLA_HEADS, QK_PAD, t), BF16),
        jax.ShapeDtypeStruct((b, MLA_HEADS, t, QK_PAD), BF16),
        jax.ShapeDtypeStruct((b, MLA_HEADS, V_ROWS, t), BF16),
        jax.ShapeDtypeStruct((b, POOL_WIDTH, t), F32),
    )
    out_specs = (
        pl.BlockSpec((1, GQA_HEADS, QK_PAD, tm), lambda bi, ti: (bi, 0, 0, ti)),
        pl.BlockSpec((1, 1, tm, QK_PAD), lambda bi, ti: (bi, 0, ti, 0)),
        pl.BlockSpec((1, GQA_KV_HEADS, V_ROWS, tm), lambda bi, ti: (bi, 0, 0, ti)),
        pl.BlockSpec((1, MLA_HEADS, QK_PAD, tm), lambda bi, ti: (bi, 0, 0, ti)),
        pl.BlockSpec((1, MLA_HEADS, tm, QK_PAD), lambda bi, ti: (bi, 0, ti, 0)),
        pl.BlockSpec((1, MLA_HEADS, V_ROWS, tm), lambda bi, ti: (bi, 0, 0, ti)),
        pl.BlockSpec((1, POOL_WIDTH, tm), lambda bi, ti: (bi, 0, ti)),
    )
    return pl.pallas_call(
        _proj_kernel,
        grid=(b, nt),
        in_specs=[
            pl.BlockSpec((1, d, tm), lambda bi, ti: (bi, 0, ti)),
            mod_spec(0), mod_spec(1),
            const_spec(gn1), const_spec(w_inT), const_spec(gpack),
            pl.BlockSpec((_R_ROWS, tm), lambda bi, ti: (0, ti)),
            const_spec(w_uqT), const_spec(w_ukvT),
        ],
        out_specs=out_specs,
        out_shape=out_shapes,
        compiler_params=pltpu.CompilerParams(
            dimension_semantics=("arbitrary", "arbitrary"), vmem_limit_bytes=VMEM_LIMIT),
        name="proj",
    )(xT, modT, modT, gn1, w_inT, gpack, ropeT, w_uqT, w_ukvT)


def _attn_kernel(flag_ref, q_ref, k_ref, v_ref, o_ref, acc_ref, l_ref, *, shared_kv, n_ctx_tiles,
                 ctx_len, seq_len, kv_chunk):
    heads = q_ref.shape[1]
    tq = q_ref.shape[3]
    is_ctx = pl.program_id(2) < n_ctx_tiles

    def kv(g):
        return 0 if shared_kv else g

    bounded = flag_ref[0] != 0

    def block(start, size, first):
        started = set()

        def finish_chain(g, lo, s):
            p = jnp.exp2(s)
            p_sum = jnp.sum(p.reshape(KV_SUB // SUBLANES, SUBLANES, tq), axis=0)
            pv = _dot(v_ref[0, kv(g), :, pl.ds(lo, KV_SUB)], p.astype(BF16))
            if first and g not in started:
                acc_ref[g] = pv
                l_ref[g] = p_sum
            else:
                acc_ref[g] += pv
                l_ref[g] += p_sum
            started.add(g)

        pending = []
        for c in range(size // KV_SUB):
            lo = start + c * KV_SUB
            for g in range(heads):
                s = _dot(k_ref[0, kv(g), pl.ds(lo, KV_SUB), :], q_ref[0, g])
                pending.append((g, lo, s))
                if len(pending) > CHAIN_SKEW:
                    finish_chain(*pending.pop(0))
        for chain in pending:
            finish_chain(*chain)

    @pl.when(jnp.logical_and(bounded, is_ctx))
    def _bounded_scores_ctx():
        block(0, ctx_len, True)

    @pl.when(jnp.logical_and(bounded, jnp.logical_not(is_ctx)))
    def _bounded_scores_latent():
        block(0, kv_chunk, True)

        def step(j, carry):
            block(pl.multiple_of(j * kv_chunk, kv_chunk), kv_chunk, False)
            return carry

        lax.fori_loop(1, (ctx_len + seq_len) // kv_chunk, step, 0)

    @pl.when(jnp.logical_not(bounded))
    def _online_softmax():
        chunk = KV_CHUNK_ONLINE
        n = jnp.where(is_ctx, ctx_len // chunk, (ctx_len + seq_len) // chunk)
        for g in range(heads):
            q = q_ref[0, g]

            def step(j, carry, g=g, q=q):
                m, l, acc = carry
                start = pl.multiple_of(j * chunk, chunk)
                s = _dot(k_ref[0, kv(g), pl.ds(start, chunk), :], q)
                m_new = jnp.maximum(m, jnp.max(s, axis=0, keepdims=True))
                alpha = jnp.exp2(m - m_new)
                p = jnp.exp2(s - m_new)
                pv = _dot(v_ref[0, kv(g), :, pl.ds(start, chunk)], p.astype(BF16))
                return m_new, alpha * l + jnp.sum(p, axis=0, keepdims=True), alpha * acc + pv

            init = (jnp.full((1, tq), -jnp.inf, F32), jnp.zeros((1, tq), F32),
                    jnp.zeros((V_ROWS, tq), F32))
            _, l, acc = lax.fori_loop(0, n, step, init)
            acc_ref[g] = acc
            l_ref[g] = jnp.concatenate([l, jnp.zeros((SUBLANES - 1, tq), F32)], axis=0)

    for g in range(heads):
        l = jnp.sum(l_ref[g], axis=0, keepdims=True)
        o_ref[0, g * HEAD_DIM:(g + 1) * HEAD_DIM] = (acc_ref[g] / l).astype(BF16)


def _kv_chunk(t):
    return max(c for c in range(Q_TILE, min(t, KV_CHUNK_MAX) + 1, Q_TILE) if t % c == 0)


def _attn_call(flag, qT, k, vT, shared_kv, n_ctx_tiles, ctx_len, name):
    b, nh, _, t = qT.shape
    tq = Q_TILE
    g = HEADS_PER_STEP
    nkv = 1 if shared_kv else g
    kern = functools.partial(_attn_kernel, shared_kv=shared_kv, n_ctx_tiles=n_ctx_tiles,
                             ctx_len=ctx_len, seq_len=t - ctx_len, kv_chunk=_kv_chunk(t))
    k_map = ((lambda bi, gi, qi, f: (bi, 0, 0, 0)) if shared_kv
             else (lambda bi, gi, qi, f: (bi, gi, 0, 0)))
    grid_spec = pltpu.PrefetchScalarGridSpec(
        num_scalar_prefetch=1,
        grid=(b, nh // g, t // tq),
        in_specs=[
            pl.BlockSpec((1, g, QK_PAD, tq), lambda bi, gi, qi, f: (bi, gi, 0, qi)),
            pl.BlockSpec((1, nkv, t, QK_PAD), k_map),
            pl.BlockSpec((1, nkv, V_ROWS, t), lambda bi, gi, qi, f: (bi, gi, 0, 0)),
        ],
        out_specs=pl.BlockSpec((1, g * HEAD_DIM, tq), lambda bi, gi, qi, f: (bi, gi, qi)),
        scratch_shapes=[pltpu.VMEM((g, V_ROWS, tq), F32), pltpu.VMEM((g, SUBLANES, tq), F32)],
    )
    return pl.pallas_call(
        kern,
        grid_spec=grid_spec,
        out_shape=jax.ShapeDtypeStruct((b, nh * HEAD_DIM, t), BF16),
        compiler_params=pltpu.CompilerParams(
            dimension_semantics=("arbitrary", "arbitrary", "arbitrary"),
            vmem_limit_bytes=VMEM_LIMIT),
        name=name,
    )(flag, qT, k, vT)


def _mix_mlp_kernel(x_ref, og_ref, om_ref, pp_ref, pc_ref, pn_ref, gt1_ref, sh2_ref, sc2_ref,
                    gt2_ref, gn_ref, ls_ref, wp_ref, wo_ref, w1_ref, w2_ref, o_ref,
                    *, n_ctx_tiles, n_tiles, token_major_out):
    tm = x_ref.shape[2]
    wide = functools.partial(_wide, width=tm)
    ti = pl.program_id(1)

    n_og = og_ref.shape[1]
    n_om = om_ref.shape[1]
    half = wo_ref.shape[1] // 2
    y_lo = (_dot(wo_ref[0, :half, :n_og], og_ref[0])
            + _dot(wo_ref[0, :half, n_og:n_og + n_om], om_ref[0]))
    y_hi = (_dot(wo_ref[0, half:, :n_og], og_ref[0])
            + _dot(wo_ref[0, half:, n_og:n_og + n_om], om_ref[0]))

    cur = pc_ref[0]
    prev_ok = jnp.logical_and(ti != 0, ti != n_ctx_tiles)
    next_ok = jnp.logical_and(ti != n_ctx_tiles - 1, ti != n_tiles - 1)
    prev = jnp.where(prev_ok, pp_ref[0][:, tm - LANES:], 0.0)
    nxt = jnp.where(next_ok, pn_ref[0][:, :LANES], 0.0)
    ext = jnp.concatenate([prev, cur, nxt], axis=1)
    width = tm + 2 * LANES

    def shifted(a, s):
        return pltpu.roll(a, s % width, axis=1)

    in_ctx = ti < n_ctx_tiles
    seg_start = jnp.where(in_ctx, 0, n_ctx_tiles * tm)
    seg_len = jnp.where(in_ctx, n_ctx_tiles * tm, (n_tiles - n_ctx_tiles) * tm)
    pos = lax.broadcasted_iota(jnp.int32, (1, tm), 1) + (ti * tm - seg_start)
    gd = POOL_GROUP_DIM
    run = ext + shifted(ext, 1)
    outs = []
    for gi, w in enumerate(POOL_WINDOWS):
        if gi > 0:
            quarter = w // 4
            run = run[gd:]
            run = shifted(run, quarter) + shifted(run, -quarter)
        cnt = jnp.minimum(pos + w // 2, seg_len) - jnp.maximum(pos - w // 2, 0)
        mean = run[:gd, LANES:LANES + tm] / cnt.astype(F32)
        dlt = (mean - cur[gi * gd:(gi + 1) * gd]).astype(BF16)
        outs.append(_dot(wp_ref[0, gi], dlt))
    op = jnp.concatenate(outs, axis=0) * wide(ls_ref[0])

    op = op.astype(BF16)
    y = jnp.concatenate([y_lo + _dot(wo_ref[0, :half, n_og + n_om:], op),
                         y_hi + _dot(wo_ref[0, half:, n_og + n_om:], op)], axis=0)
    x1 =x_ref[0] + wide(gt1_ref[0, 0, 0]) * y

    d_model = x1.shape[0]
    xn = x1 * _rms_rows(x1, d_model) * wide(gn_ref[0])
    h2 = (xn * (1.0 + wide(sc2_ref[0, 0, 0])) + wide(sh2_ref[0, 0, 0])).astype(BF16)
    d_ff = w1_ref.shape[1]
    n_chunks = d_ff // FF_CHUNK

    def up(c):
        a = jnp.maximum(_dot(w1_ref[0, c * FF_CHUNK:(c + 1) * FF_CHUNK, :], h2), 0.0)
        return (a * a).astype(BF16)

    y2 = None
    hidden = up(0)
    for c in range(n_chunks):
        nxt = up(c + 1) if c + 1 < n_chunks else None
        part = _dot(w2_ref[0, :, c * FF_CHUNK:(c + 1) * FF_CHUNK], hidden)
        y2 = part if y2 is None else y2 + part
        hidden = nxt
    x2 = x1 + wide(gt2_ref[0, 0, 0]) * y2
    o_ref[0] = x2.T if token_major_out else x2


def _mix_mlp_call(layer, xT, ogT, omT, poolT, modT, gn2, ls_rep, w_poolT, w_outT, w1T, w2T,
                  n_ctx_tiles, final):
    b, d, t = xT.shape
    tm = TOKEN_TILE
    nt = t // tm
    nb = modT.shape[1] - 1

    def mod_spec(slot):
        return pl.BlockSpec((1, 1, 1, d, LANES),
                            lambda bi, ti: (layer, jnp.where(ti < n_ctx_tiles, nb, bi), slot, 0, 0))

    def const_spec(arr):
        return pl.BlockSpec((1,) + arr.shape[1:], lambda bi, ti: (layer,) + (0,) * (arr.ndim - 1))

    def tok_spec(rows):
        return pl.BlockSpec((1, rows, tm), lambda bi, ti: (bi, 0, ti))

    kern = functools.partial(_mix_mlp_kernel, n_ctx_tiles=n_ctx_tiles, n_tiles=nt,
                             token_major_out=final)
    if final:
        out_spec = pl.BlockSpec((1, tm, d), lambda bi, ti: (bi, jnp.maximum(ti - n_ctx_tiles, 0), 0))
        out_shape = jax.ShapeDtypeStruct((b, t - n_ctx_tiles * tm, d), F32)
    else:
        out_spec = tok_spec(d)
        out_shape = jax.ShapeDtypeStruct((b, d, t), F32)
    return pl.pallas_call(
        kern,
        grid=(b, nt),
        in_specs=[
            tok_spec(d), tok_spec(ogT.shape[1]), tok_spec(omT.shape[1]),
            pl.BlockSpec((1, POOL_WIDTH, tm), lambda bi, ti: (bi, 0, jnp.maximum(ti - 1, 0))),
            tok_spec(POOL_WIDTH),
            pl.BlockSpec((1, POOL_WIDTH, tm), lambda bi, ti: (bi, 0, jnp.minimum(ti + 1, nt - 1))),
            mod_spec(2), mod_spec(3), mod_spec(4), mod_spec(5),
            const_spec(gn2), const_spec(ls_rep), const_spec(w_poolT), const_spec(w_outT),
            const_spec(w1T), const_spec(w2T),
        ],
        out_specs=out_spec,
        out_shape=out_shape,
        compiler_params=pltpu.CompilerParams(
            dimension_semantics=("arbitrary", "arbitrary"), vmem_limit_bytes=VMEM_LIMIT),
        name="mix_mlp",
    )(xT, ogT, omT, poolT, poolT, poolT, modT, modT, modT, modT, gn2, ls_rep, w_poolT, w_outT,
      w1T, w2T)


def _rope_table(seq_len, ctx_len):
    rows = seq_len // GRID_W
    row = jnp.repeat(jnp.arange(rows, dtype=F32), GRID_W)
    col = jnp.tile(jnp.arange(GRID_W, dtype=F32), rows)

    def block(rot_dim):
        n = rot_dim // 4
        inv = ROPE_THETA ** (-jnp.arange(n, dtype=F32) / n)
        ang = jnp.concatenate([row[:, None] * inv, col[:, None] * inv], axis=-1)
        cos, sin = jnp.cos(ang).T, jnp.sin(ang).T
        c = jnp.concatenate([cos, cos], axis=0)
        s = jnp.concatenate([-sin, sin], axis=0)
        c = jnp.concatenate([jnp.ones((rot_dim, ctx_len), F32), c], axis=1)
        s = jnp.concatenate([jnp.zeros((rot_dim, ctx_len), F32), s], axis=1)
        return c, s

    cg, sg = block(HEAD_DIM)
    cm, sm = block(MLA_ROPE)
    return jnp.concatenate([cg, sg, cm, sm], axis=0)


def _lane_rep(a):
    return jnp.broadcast_to(a[..., None], a.shape + (LANES,)).astype(F32)


def _swap_half(n):
    return (np.arange(n) + n // 2) % n


def _prep_weights(w_mod, b_mod, g_norm1, g_norm2, w_in, g_q_gqa, g_k_gqa, g_cq, g_ckv, w_uq, w_ukv,
                  g_q_mla, g_k_mla, w_pool, ls_pool, w_out, w_mlp1, w_mlp2):
    o_cq = (GQA_HEADS + 2 * GQA_KV_HEADS) * HEAD_DIM
    o_pool = o_cq + MLA_Q_RANK + MLA_KV_RANK + MLA_ROPE
    sw64 = _swap_half(HEAD_DIM)
    sw32 = _swap_half(MLA_ROPE)
    w_in_ord = jnp.concatenate([w_in[:, :, o_cq:o_pool], w_in[:, :, :o_cq], w_in[:, :, o_pool:]], axis=2)
    assert w_in_ord.shape[2] == _IN_EXT
    w_inT = jnp.transpose(w_in_ord, (0, 2, 1)).astype(BF16)
    w_uqT = jnp.transpose(w_uq, (0, 2, 1)).astype(BF16)
    w_ukvT = jnp.transpose(w_ukv, (0, 2, 1)).astype(BF16)

    gq_r, gk_r = g_q_mla[:, MLA_NOPE:], g_k_mla[:, MLA_NOPE:]
    gpack = jnp.concatenate([
        g_q_gqa, g_q_gqa[:, sw64], g_k_gqa, g_k_gqa[:, sw64], g_cq, g_ckv,
        g_q_mla[:, :MLA_NOPE], gq_r, gq_r[:, sw32], g_k_mla[:, :MLA_NOPE], gk_r, gk_r[:, sw32]], axis=1)
    assert gpack.shape[1] == _G_ROWS
    return dict(
        w_modT=jnp.transpose(w_mod, (0, 2, 1)).astype(BF16),
        b_rep=_lane_rep(b_mod),
        gn1=_lane_rep(g_norm1), gn2=_lane_rep(g_norm2),
        w_inT=w_inT, w_uqT=w_uqT, w_ukvT=w_ukvT, gpack=_lane_rep(gpack),
        w_poolT=jnp.transpose(w_pool, (0, 1, 3, 2)).astype(BF16),
        ls_rep=_lane_rep(ls_pool),
        w_outT=jnp.transpose(w_out, (0, 2, 1)).astype(BF16),
        w1T=jnp.transpose(w_mlp1, (0, 2, 1)).astype(BF16),
        w2T=jnp.transpose(w_mlp2, (0, 2, 1)).astype(BF16),
    )


def kernel(x, c, ctx, c_ctx, w_mod, b_mod, g_norm1, g_norm2, w_in, g_q_gqa, g_k_gqa, g_cq, g_ckv, w_uq, w_ukv, g_q_mla, g_k_mla, w_pool, ls_pool, w_out, w_mlp1, w_mlp2):
    batch, seq, d_model = x.shape
    ctx_len = ctx.shape[1]
    depth = w_in.shape[0]
    assert ctx_len % TOKEN_TILE == 0 and seq % TOKEN_TILE == 0 and seq % GRID_W == 0
    assert TOKEN_TILE == Q_TILE
    assert ctx_len % KV_CHUNK_ONLINE == 0 and seq % KV_CHUNK_ONLINE == 0
    assert GQA_GROUP == HEADS_PER_STEP and MLA_HEADS % HEADS_PER_STEP == 0
    n_ctx_tiles = ctx_len // TOKEN_TILE

    def bounded_flag(g_q, g_k, dim):
        bound = dim ** 0.5 * LOG2E * jnp.max(jnp.abs(g_q), axis=1) * jnp.max(jnp.abs(g_k), axis=1)
        return (bound <= SAFE_LOG2_BOUND).astype(jnp.int32)

    flag_g = bounded_flag(g_q_gqa, g_k_gqa, HEAD_DIM)
    flag_m = bounded_flag(g_q_mla, g_k_mla, MLA_QK)

    p = _prep_weights(w_mod, b_mod, g_norm1, g_norm2, w_in, g_q_gqa, g_k_gqa, g_cq, g_ckv, w_uq,
                      w_ukv, g_q_mla, g_k_mla, w_pool, ls_pool, w_out, w_mlp1, w_mlp2)
    ropeT = _rope_table(seq, ctx_len)
    c_rep = _lane_rep(jnp.concatenate([c, c_ctx[None, :]], axis=0))
    modT = _mod_call(p["w_modT"], c_rep, p["b_rep"])

    xT = _to_feature_major_call(ctx, x, n_ctx_tiles)
    for i in range(depth):
        qg, kg, vg, qm, km, vm, pool_in = _proj_call(
            i, xT, modT, p["gn1"], p["w_inT"], p["gpack"], ropeT, p["w_uqT"], p["w_ukvT"], n_ctx_tiles)
        ogT = _attn_call(flag_g[i:i + 1], qg, kg, vg, True, n_ctx_tiles, ctx_len, "attn_gqa")
        omT = _attn_call(flag_m[i:i + 1], qm, km, vm, False, n_ctx_tiles, ctx_len, "attn_mla")
        xT = _mix_mlp_call(i, xT, ogT, omT, pool_in, modT, p["gn2"], p["ls_rep"], p["w_poolT"],
                           p["w_outT"], p["w1T"], p["w2T"], n_ctx_tiles, final=i == depth - 1)
    return xT
```

```python
import functools
import math

import numpy as np
import jax
import jax.numpy as jnp
from jax import lax
from jax.experimental import pallas as pl
from jax.experimental.pallas import tpu as pltpu

F32 = jnp.float32
BF16 = jnp.bfloat16

GRID_W = 64
ROPE_THETA = 10000.0
NORM_EPS = 1e-6
GQA_HEADS = 6
GQA_KV_HEADS = 2
GQA_GROUP = GQA_HEADS // GQA_KV_HEADS
HEAD_DIM = 64
MLA_HEADS = 6
MLA_NOPE = 64
MLA_ROPE = 32
MLA_QK = MLA_NOPE + MLA_ROPE
MLA_V = 64
MLA_Q_RANK = 256
MLA_KV_RANK = 128
POOL_WINDOWS = (2, 4, 8, 16)
POOL_GROUP_DIM = 64
POOL_WIDTH = len(POOL_WINDOWS) * POOL_GROUP_DIM
N_MOD = 6

LANES = 128
QK_PAD = 128
TOKEN_TILE = 256
Q_TILE = 256
KV_CHUNK_MAX = 8448
KV_SUB = 256
CHAIN_SKEW = 5
KV_CHUNK_ONLINE = 256
HEADS_PER_STEP = 6
V_ROWS = HEAD_DIM
SUBLANES = 8
FF_CHUNK = 256
VMEM_LIMIT = 56 * 1024 * 1024
LOG2E = 1.4426950408889634
SAFE_LOG2_BOUND = 64.0

_L_CQ = 0
_L_CKV = _L_CQ + MLA_Q_RANK
_L_KR = _L_CKV + MLA_KV_RANK
_N_LAT = _L_KR + MLA_ROPE
_O_Q = 0
_O_K = _O_Q + GQA_HEADS * HEAD_DIM
_O_V = _O_K + GQA_KV_HEADS * HEAD_DIM
_O_POOL = _O_V + GQA_KV_HEADS * HEAD_DIM
_IN_EXT = _N_LAT + _O_POOL + POOL_WIDTH

_G_Q = 0
_G_QS = _G_Q + HEAD_DIM
_G_K = _G_QS + HEAD_DIM
_G_KS = _G_K + HEAD_DIM
_G_CQ = _G_KS + HEAD_DIM
_G_CKV = _G_CQ + MLA_Q_RANK
_G_QM_N = _G_CKV + MLA_KV_RANK
_G_QM_R = _G_QM_N + MLA_NOPE
_G_QM_RS = _G_QM_R + MLA_ROPE
_G_KM_N = _G_QM_RS + MLA_ROPE
_G_KM_R = _G_KM_N + MLA_NOPE
_G_KM_RS = _G_KM_R + MLA_ROPE
_G_ROWS = _G_KM_RS + MLA_ROPE

_R_CG = 0
_R_SG = _R_CG + HEAD_DIM
_R_CM = _R_SG + HEAD_DIM
_R_SM = _R_CM + MLA_ROPE
_R_ROWS = _R_SM + MLA_ROPE


def _wide(a, width):
    return jnp.concatenate([a] * (width // LANES), axis=1)


def _dot(a, b):
    return jnp.dot(a, b, preferred_element_type=F32)


def _to_feature_major_kernel(ctx_ref, x_ref, o_ref, *, n_ctx_tiles):
    is_ctx = pl.program_id(1) < n_ctx_tiles

    @pl.when(is_ctx)
    def _():
        o_ref[0] = ctx_ref[0].T

    @pl.when(jnp.logical_not(is_ctx))
    def _():
        o_ref[0] = x_ref[0].T


def _to_feature_major_call(ctx, x, n_ctx_tiles):
    b, seq, d = x.shape
    tm = TOKEN_TILE
    nt = n_ctx_tiles + seq // tm
    return pl.pallas_call(
        functools.partial(_to_feature_major_kernel, n_ctx_tiles=n_ctx_tiles),
        grid=(b, nt),
        in_specs=[
            pl.BlockSpec((1, tm, d), lambda bi, ti: (bi, jnp.minimum(ti, n_ctx_tiles - 1), 0)),
            pl.BlockSpec((1, tm, d), lambda bi, ti: (bi, jnp.maximum(ti - n_ctx_tiles, 0), 0)),
        ],
        out_specs=pl.BlockSpec((1, d, tm), lambda bi, ti: (bi, 0, ti)),
        out_shape=jax.ShapeDtypeStruct((b, d, nt * tm), F32),
        compiler_params=pltpu.CompilerParams(
            dimension_semantics=("arbitrary", "arbitrary"), vmem_limit_bytes=VMEM_LIMIT),
        name="to_feature_major",
    )(ctx, x)


def _mod_kernel(w_ref, c_ref, b_ref, o_ref):
    w = w_ref[0]
    b = b_ref[0]
    for j in range(c_ref.shape[0]):
        c = c_ref[j]
        s = c / (1.0 + jnp.exp(-c))
        o_ref[0, j, 0] = _dot(w, s.astype(BF16)) + b


def _mod_call(w_modT, c_rep, b_rep):
    depth, six_d, d = w_modT.shape
    nvec = c_rep.shape[0]
    return pl.pallas_call(
        _mod_kernel,
        grid=(depth, N_MOD),
        in_specs=[
            pl.BlockSpec((1, d, d), lambda i, n: (i, n, 0)),
            pl.BlockSpec((nvec, d, LANES), lambda i, n: (0, 0, 0)),
            pl.BlockSpec((1, d, LANES), lambda i, n: (i, n, 0)),
        ],
        out_specs=pl.BlockSpec((1, nvec, 1, d, LANES), lambda i, n: (i, 0, n, 0, 0)),
        out_shape=jax.ShapeDtypeStruct((depth, nvec, N_MOD, d, LANES), F32),
        compiler_params=pltpu.CompilerParams(
            dimension_semantics=("arbitrary", "arbitrary"), vmem_limit_bytes=VMEM_LIMIT),
        name="mod_vectors",
    )(w_modT, c_rep, b_rep)


def _swap_halves(a):
    half = a.shape[0] // 2
    return jnp.concatenate([a[half:], a[:half]], axis=0)


def _rms_rows(a, n):
    return lax.rsqrt(jnp.sum(a * a, axis=0, keepdims=True) * (1.0 / n) + NORM_EPS)


def _proj_kernel(x_ref, sh_ref, sc_ref, gn_ref, w_ref, gp_ref, rope_ref, wuq_ref, wukv_ref,
                 qg_ref, kg_ref, vg_ref, qm_ref, km_ref, vm_ref, pool_ref):
    tm = x_ref.shape[2]
    wide = functools.partial(_wide, width=tm)
    x = x_ref[0]
    d_model = x.shape[0]
    xn = x * _rms_rows(x, d_model) * wide(gn_ref[0])
    h = xn * (1.0 + wide(sc_ref[0, 0, 0])) + wide(sh_ref[0, 0, 0])
    hb = h.astype(BF16)
    gp = gp_ref[0]

    def gain(off, n):
        return wide(gp[off:off + n])

    u_lat = _dot(w_ref[0, :_N_LAT, :], hb)
    u = _dot(w_ref[0, _N_LAT:, :], hb)
    cq = u_lat[_L_CQ:_L_CQ + MLA_Q_RANK]
    cq_n = cq * _rms_rows(cq, MLA_Q_RANK) * gain(_G_CQ, MLA_Q_RANK)
    qm_all = _dot(wuq_ref[0], cq_n.astype(BF16))
    ckv = u_lat[_L_CKV:_L_CKV + MLA_KV_RANK]
    ckv_n = ckv * _rms_rows(ckv, MLA_KV_RANK) * gain(_G_CKV, MLA_KV_RANK)
    kv_all = _dot(wukv_ref[0], ckv_n.astype(BF16))
    kr = u_lat[_L_KR:_L_KR + MLA_ROPE]
    kr_sw = _swap_halves(kr)
    kr_ss = jnp.sum(kr * kr, axis=0, keepdims=True)

    cg = rope_ref[_R_CG:_R_CG + HEAD_DIM]
    sg = rope_ref[_R_SG:_R_SG + HEAD_DIM]
    cm = rope_ref[_R_CM:_R_CM + MLA_ROPE]
    sm = rope_ref[_R_SM:_R_SM + MLA_ROPE]
    zeros_kv = jnp.zeros((HEAD_DIM, tm), F32)
    zeros_pad = jnp.zeros((QK_PAD - MLA_QK, tm), F32)

    scale_g = HEAD_DIM ** -0.5 * LOG2E
    gcq = gain(_G_Q, HEAD_DIM) * cg * scale_g
    gsq = gain(_G_QS, HEAD_DIM) * sg * scale_g
    for hd in range(GQA_HEADS):
        a = u[_O_Q + hd * HEAD_DIM:_O_Q + (hd + 1) * HEAD_DIM]
        q = _rms_rows(a, HEAD_DIM) * (a * gcq + _swap_halves(a) * gsq)
        parts = [zeros_kv] * GQA_KV_HEADS
        parts[hd // GQA_GROUP] = q
        qg_ref[0, hd] = jnp.concatenate(parts, axis=0).astype(BF16)
    gck = gain(_G_K, HEAD_DIM) * cg
    gsk = gain(_G_KS, HEAD_DIM) * sg
    ks = []
    for hd in range(GQA_KV_HEADS):
        a = u[_O_K + hd * HEAD_DIM:_O_K + (hd + 1) * HEAD_DIM]
        ks.append(_rms_rows(a, HEAD_DIM) * (a * gck + _swap_halves(a) * gsk))
    kg_ref[0, 0] = jnp.concatenate(ks, axis=0).T.astype(BF16)
    for hd in range(GQA_KV_HEADS):
        vg_ref[0, hd] = u[_O_V + hd * HEAD_DIM:_O_V + (hd + 1) * HEAD_DIM].astype(BF16)

    scale_m = MLA_QK ** -0.5 * LOG2E
    gq_n = gain(_G_QM_N, MLA_NOPE) * scale_m
    gq_c = gain(_G_QM_R, MLA_ROPE) * cm * scale_m
    gq_s = gain(_G_QM_RS, MLA_ROPE) * sm * scale_m
    gk_n = gain(_G_KM_N, MLA_NOPE)
    gk_c = gain(_G_KM_R, MLA_ROPE) * cm
    gk_s = gain(_G_KM_RS, MLA_ROPE) * sm
    k_rope = kr * gk_c + kr_sw * gk_s
    for hd in range(MLA_HEADS):
        a = qm_all[hd * MLA_QK:(hd + 1) * MLA_QK]
        r = _rms_rows(a, MLA_QK)
        a_n, a_r = a[:MLA_NOPE], a[MLA_NOPE:]
        q = jnp.concatenate(
            [r * (a_n * gq_n), r * (a_r * gq_c + _swap_halves(a_r) * gq_s), zeros_pad], axis=0)
        qm_ref[0, hd] = q.astype(BF16)

        kvh = kv_all[hd * (MLA_NOPE + MLA_V):(hd + 1) * (MLA_NOPE + MLA_V)]
        kn, v = kvh[:MLA_NOPE], kvh[MLA_NOPE:]
        rk = lax.rsqrt((jnp.sum(kn * kn, axis=0, keepdims=True) + kr_ss) * (1.0 / MLA_QK) + NORM_EPS)
        k = jnp.concatenate([rk * (kn * gk_n), rk * k_rope, zeros_pad], axis=0)
        km_ref[0, hd] = k.T.astype(BF16)
        vm_ref[0, hd] = v.astype(BF16)

    pool_ref[0] = u[_O_POOL:_O_POOL + POOL_WIDTH]


def _proj_call(layer, xT, modT, gn1, w_inT, gpack, ropeT, w_uqT, w_ukvT, n_ctx_tiles):
    b, d, t = xT.shape
    tm = TOKEN_TILE
    nt = t // tm
    nb = modT.shape[1] - 1

    def mod_spec(slot):
        return pl.BlockSpec((1, 1, 1, d, LANES),
                            lambda bi, ti: (layer, jnp.where(ti < n_ctx_tiles, nb, bi), slot, 0, 0))

    def const_spec(arr):
        return pl.BlockSpec((1,) + arr.shape[1:], lambda bi, ti: (layer,) + (0,) * (arr.ndim - 1))

    out_shapes = (
        jax.ShapeDtypeStruct((b, GQA_HEADS, QK_PAD, t), BF16),
        jax.ShapeDtypeStruct((b, 1, t, QK_PAD), BF16),
        jax.ShapeDtypeStruct((b, GQA_KV_HEADS, V_ROWS, t), BF16),
        jax.ShapeDtypeStruct((b, MLA_HEADS, QK_PAD, t), BF16),
        jax.ShapeDtypeStruct((b, MLA_HEADS, t, QK_PAD), BF16),
        jax.ShapeDtypeStruct((b, MLA_HEADS, V_ROWS, t), BF16),
        jax.ShapeDtypeStruct((b, POOL_WIDTH, t), F32),
    )
    out_specs = (
        pl.BlockSpec((1, GQA_HEADS, QK_PAD, tm), lambda bi, ti: (bi, 0, 0, ti)),
        pl.BlockSpec((1, 1, tm, QK_PAD), lambda bi, ti: (bi, 0, ti, 0)),
        pl.BlockSpec((1, GQA_KV_HEADS, V_ROWS, tm), lambda bi, ti: (bi, 0, 0, ti)),
        pl.BlockSpec((1, MLA_HEADS, QK_PAD, tm), lambda bi, ti: (bi, 0, 0, ti)),
        pl.BlockSpec((1, MLA_HEADS, tm, QK_PAD), lambda bi, ti: (bi, 0, ti, 0)),
        pl.BlockSpec((1, MLA_HEADS, V_ROWS, tm), lambda bi, ti: (bi, 0, 0, ti)),
        pl.BlockSpec((1, POOL_WIDTH, tm), lambda bi, ti: (bi, 0, ti)),
    )
    return pl.pallas_call(
        _proj_kernel,
        grid=(b, nt),
        in_specs=[
            pl.BlockSpec((1, d, tm), lambda bi, ti: (bi, 0, ti)),
            mod_spec(0), mod_spec(1),
            const_spec(gn1), const_spec(w_inT), const_spec(gpack),
            pl.BlockSpec((_R_ROWS, tm), lambda bi, ti: (0, ti)),
            const_spec(w_uqT), const_spec(w_ukvT),
        ],
        out_specs=out_specs,
        out_shape=out_shapes,
        compiler_params=pltpu.CompilerParams(
            dimension_semantics=("arbitrary", "arbitrary"), vmem_limit_bytes=VMEM_LIMIT),
        name="proj",
    )(xT, modT, modT, gn1, w_inT, gpack, ropeT, w_uqT, w_ukvT)


def _attn_kernel(flag_ref, q_ref, k_ref, v_ref, o_ref, acc_ref, l_ref, *, q_per_kv, shared_k,
                 n_ctx_tiles, ctx_len, seq_len, kv_chunk):
    heads = q_ref.shape[1]
    tq = q_ref.shape[3]
    is_ctx = pl.program_id(2) < n_ctx_tiles

    def k_of(g):
        return 0 if shared_k else g // q_per_kv

    def v_of(g):
        return g // q_per_kv

    bounded = flag_ref[0] != 0

    def block(start, size, first):
        started = set()

        def finish_chain(g, lo, s):
            p = jnp.exp2(s)
            p_sum = jnp.sum(p.reshape(KV_SUB // SUBLANES, SUBLANES, tq), axis=0)
            pv = _dot(v_ref[0, v_of(g), :, pl.ds(lo, KV_SUB)], p.astype(BF16))
            if first and g not in started:
                acc_ref[g] = pv
                l_ref[g] = p_sum
            else:
                acc_ref[g] += pv
                l_ref[g] += p_sum
            started.add(g)

        pending = []
        for c in range(size // KV_SUB):
            lo = start + c * KV_SUB
            for g in range(heads):
                s = _dot(k_ref[0, k_of(g), pl.ds(lo, KV_SUB), :], q_ref[0, g])
                pending.append((g, lo, s))
                if len(pending) > CHAIN_SKEW:
                    finish_chain(*pending.pop(0))
        for chain in pending:
            finish_chain(*chain)

    @pl.when(jnp.logical_and(bounded, is_ctx))
    def _bounded_scores_ctx():
        block(0, ctx_len, True)

    @pl.when(jnp.logical_and(bounded, jnp.logical_not(is_ctx)))
    def _bounded_scores_latent():
        block(0, kv_chunk, True)

        def step(j, carry):
            block(pl.multiple_of(j * kv_chunk, kv_chunk), kv_chunk, False)
            return carry

        lax.fori_loop(1, (ctx_len + seq_len) // kv_chunk, step, 0)

    @pl.when(jnp.logical_not(bounded))
    def _online_softmax():
        chunk = KV_CHUNK_ONLINE
        n = jnp.where(is_ctx, ctx_len // chunk, (ctx_len + seq_len) // chunk)
        for g in range(heads):
            q = q_ref[0, g]

            def step(j, carry, g=g, q=q):
                m, l, acc = carry
                start = pl.multiple_of(j * chunk, chunk)
                s = _dot(k_ref[0, k_of(g), pl.ds(start, chunk), :], q)
                m_new = jnp.maximum(m, jnp.max(s, axis=0, keepdims=True))
                alpha = jnp.exp2(m - m_new)
                p = jnp.exp2(s - m_new)
                pv = _dot(v_ref[0, v_of(g), :, pl.ds(start, chunk)], p.astype(BF16))
                return m_new, alpha * l + jnp.sum(p, axis=0, keepdims=True), alpha * acc + pv

            init = (jnp.full((1, tq), -jnp.inf, F32), jnp.zeros((1, tq), F32),
                    jnp.zeros((V_ROWS, tq), F32))
            _, l, acc = lax.fori_loop(0, n, step, init)
            acc_ref[g] = acc
            l_ref[g] = jnp.concatenate([l, jnp.zeros((SUBLANES - 1, tq), F32)], axis=0)

    for g in range(heads):
        l = jnp.sum(l_ref[g], axis=0, keepdims=True)
        o_ref[0, g * HEAD_DIM:(g + 1) * HEAD_DIM] = (acc_ref[g] / l).astype(BF16)


def _kv_chunk(t):
    return max(c for c in range(Q_TILE, min(t, KV_CHUNK_MAX) + 1, Q_TILE) if t % c == 0)


def _attn_call(flag, qT, k, vT, q_per_kv, shared_k, n_ctx_tiles, ctx_len, name):
    b, nh, _, t = qT.shape
    tq = Q_TILE
    g = HEADS_PER_STEP
    nv = g // q_per_kv
    nk = 1 if shared_k else nv
    kern = functools.partial(_attn_kernel, q_per_kv=q_per_kv, shared_k=shared_k,
                             n_ctx_tiles=n_ctx_tiles, ctx_len=ctx_len, seq_len=t - ctx_len,
                             kv_chunk=_kv_chunk(t))
    k_map = ((lambda bi, gi, qi, f: (bi, 0, 0, 0)) if shared_k
             else (lambda bi, gi, qi, f: (bi, gi, 0, 0)))
    grid_spec = pltpu.PrefetchScalarGridSpec(
        num_scalar_prefetch=1,
        grid=(b, nh // g, t // tq),
        in_specs=[
            pl.BlockSpec((1, g, QK_PAD, tq), lambda bi, gi, qi, f: (bi, gi, 0, qi)),
            pl.BlockSpec((1, nk, t, QK_PAD), k_map),
            pl.BlockSpec((1, nv, V_ROWS, t), lambda bi, gi, qi, f: (bi, gi, 0, 0)),
        ],
        out_specs=pl.BlockSpec((1, g * HEAD_DIM, tq), lambda bi, gi, qi, f: (bi, gi, qi)),
        scratch_shapes=[pltpu.VMEM((g, V_ROWS, tq), F32), pltpu.VMEM((g, SUBLANES, tq), F32)],
    )
    return pl.pallas_call(
        kern,
        grid_spec=grid_spec,
        out_shape=jax.ShapeDtypeStruct((b, nh * HEAD_DIM, t), BF16),
        compiler_params=pltpu.CompilerParams(
            dimension_semantics=("arbitrary", "arbitrary", "arbitrary"),
            vmem_limit_bytes=VMEM_LIMIT),
        name=name,
    )(flag, qT, k, vT)


def _mix_mlp_kernel(x_ref, og_ref, om_ref, pp_ref, pc_ref, pn_ref, gt1_ref, sh2_ref, sc2_ref,
                    gt2_ref, gn_ref, ls_ref, wp_ref, wo_ref, w1_ref, w2_ref, o_ref,
                    *, n_ctx_tiles, n_tiles, token_major_out):
    tm = x_ref.shape[2]
    wide = functools.partial(_wide, width=tm)
    ti = pl.program_id(1)

    n_og = og_ref.shape[1]
    n_om = om_ref.shape[1]
    half = wo_ref.shape[1] // 2
    y_lo = (_dot(wo_ref[0, :half, :n_og], og_ref[0])
            + _dot(wo_ref[0, :half, n_og:n_og + n_om], om_ref[0]))
    y_hi = (_dot(wo_ref[0, half:, :n_og], og_ref[0])
            + _dot(wo_ref[0, half:, n_og:n_og + n_om], om_ref[0]))

    cur = pc_ref[0]
    prev_ok = jnp.logical_and(ti != 0, ti != n_ctx_tiles)
    next_ok = jnp.logical_and(ti != n_ctx_tiles - 1, ti != n_tiles - 1)
    prev = jnp.where(prev_ok, pp_ref[0][:, tm - LANES:], 0.0)
    nxt = jnp.where(next_ok, pn_ref[0][:, :LANES], 0.0)
    ext = jnp.concatenate([prev, cur, nxt], axis=1)
    width = tm + 2 * LANES

    def shifted(a, s):
        return pltpu.roll(a, s % width, axis=1)

    in_ctx = ti < n_ctx_tiles
    seg_start = jnp.where(in_ctx, 0, n_ctx_tiles * tm)
    seg_len = jnp.where(in_ctx, n_ctx_tiles * tm, (n_tiles - n_ctx_tiles) * tm)
    pos = lax.broadcasted_iota(jnp.int32, (1, tm), 1) + (ti * tm - seg_start)
    gd = POOL_GROUP_DIM
    run = ext + shifted(ext, 1)
    outs = []
    for gi, w in enumerate(POOL_WINDOWS):
        if gi > 0:
            quarter = w // 4
            run = run[gd:]
            run = shifted(run, quarter) + shifted(run, -quarter)
        cnt = jnp.minimum(pos + w // 2, seg_len) - jnp.maximum(pos - w // 2, 0)
        mean = run[:gd, LANES:LANES + tm] / cnt.astype(F32)
        dlt = (mean - cur[gi * gd:(gi + 1) * gd]).astype(BF16)
        outs.append(_dot(wp_ref[0, gi], dlt))
    op = jnp.concatenate(outs, axis=0) * wide(ls_ref[0])

    op = op.astype(BF16)
    y = jnp.concatenate([y_lo + _dot(wo_ref[0, :half, n_og + n_om:], op),
                         y_hi + _dot(wo_ref[0, half:, n_og + n_om:], op)], axis=0)
    x1 =x_ref[0] + wide(gt1_ref[0, 0, 0]) * y

    d_model = x1.shape[0]
    xn = x1 * _rms_rows(x1, d_model) * wide(gn_ref[0])
    h2 = (xn * (1.0 + wide(sc2_ref[0, 0, 0])) + wide(sh2_ref[0, 0, 0])).astype(BF16)
    d_ff = w1_ref.shape[1]
    n_chunks = d_ff // FF_CHUNK

    def up(c):
        a = jnp.maximum(_dot(w1_ref[0, c * FF_CHUNK:(c + 1) * FF_CHUNK, :], h2), 0.0)
        return (a * a).astype(BF16)

    y2 = None
    hidden = up(0)
    for c in range(n_chunks):
        nxt = up(c + 1) if c + 1 < n_chunks else None
        part = _dot(w2_ref[0, :, c * FF_CHUNK:(c + 1) * FF_CHUNK], hidden)
        y2 = part if y2 is None else y2 + part
        hidden = nxt
    x2 = x1 + wide(gt2_ref[0, 0, 0]) * y2
    o_ref[0] = x2.T if token_major_out else x2


def _mix_mlp_call(layer, xT, ogT, omT, poolT, modT, gn2, ls_rep, w_poolT, w_outT, w1T, w2T,
                  n_ctx_tiles, final):
    b, d, t = xT.shape
    tm = TOKEN_TILE
    nt = t // tm
    nb = modT.shape[1] - 1

    def mod_spec(slot):
        return pl.BlockSpec((1, 1, 1, d, LANES),
                            lambda bi, ti: (layer, jnp.where(ti < n_ctx_tiles, nb, bi), slot, 0, 0))

    def const_spec(arr):
        return pl.BlockSpec((1,) + arr.shape[1:], lambda bi, ti: (layer,) + (0,) * (arr.ndim - 1))

    def tok_spec(rows):
        return pl.BlockSpec((1, rows, tm), lambda bi, ti: (bi, 0, ti))

    kern = functools.partial(_mix_mlp_kernel, n_ctx_tiles=n_ctx_tiles, n_tiles=nt,
                             token_major_out=final)
    if final:
        out_spec = pl.BlockSpec((1, tm, d), lambda bi, ti: (bi, jnp.maximum(ti - n_ctx_tiles, 0), 0))
        out_shape = jax.ShapeDtypeStruct((b, t - n_ctx_tiles * tm, d), F32)
    else:
        out_spec = tok_spec(d)
        out_shape = jax.ShapeDtypeStruct((b, d, t), F32)
    return pl.pallas_call(
        kern,
        grid=(b, nt),
        in_specs=[
            tok_spec(d), tok_spec(ogT.shape[1]), tok_spec(omT.shape[1]),
            pl.BlockSpec((1, POOL_WIDTH, tm), lambda bi, ti: (bi, 0, jnp.maximum(ti - 1, 0))),
            tok_spec(POOL_WIDTH),
            pl.BlockSpec((1, POOL_WIDTH, tm), lambda bi, ti: (bi, 0, jnp.minimum(ti + 1, nt - 1))),
            mod_spec(2), mod_spec(3), mod_spec(4), mod_spec(5),
            const_spec(gn2), const_spec(ls_rep), const_spec(w_poolT), const_spec(w_outT),
            const_spec(w1T), const_spec(w2T),
        ],
        out_specs=out_spec,
        out_shape=out_shape,
        compiler_params=pltpu.CompilerParams(
            dimension_semantics=("arbitrary", "arbitrary"), vmem_limit_bytes=VMEM_LIMIT),
        name="mix_mlp",
    )(xT, ogT, omT, poolT, poolT, poolT, modT, modT, modT, modT, gn2, ls_rep, w_poolT, w_outT,
      w1T, w2T)


def _rope_table(seq_len, ctx_len):
    rows = seq_len // GRID_W
    row = jnp.repeat(jnp.arange(rows, dtype=F32), GRID_W)
    col = jnp.tile(jnp.arange(GRID_W, dtype=F32), rows)

    def block(rot_dim):
        n = rot_dim // 4
        inv = ROPE_THETA ** (-jnp.arange(n, dtype=F32) / n)
        ang = jnp.concatenate([row[:, None] * inv, col[:, None] * inv], axis=-1)
        cos, sin = jnp.cos(ang).T, jnp.sin(ang).T
        c = jnp.concatenate([cos, cos], axis=0)
        s = jnp.concatenate([-sin, sin], axis=0)
        c = jnp.concatenate([jnp.ones((rot_dim, ctx_len), F32), c], axis=1)
        s = jnp.concatenate([jnp.zeros((rot_dim, ctx_len), F32), s], axis=1)
        return c, s

    cg, sg = block(HEAD_DIM)
    cm, sm = block(MLA_ROPE)
    return jnp.concatenate([cg, sg, cm, sm], axis=0)


def _lane_rep(a):
    return jnp.broadcast_to(a[..., None], a.shape + (LANES,)).astype(F32)


def _swap_half(n):
    return (np.arange(n) + n // 2) % n


def _prep_weights(w_mod, b_mod, g_norm1, g_norm2, w_in, g_q_gqa, g_k_gqa, g_cq, g_ckv, w_uq, w_ukv,
                  g_q_mla, g_k_mla, w_pool, ls_pool, w_out, w_mlp1, w_mlp2):
    o_cq = (GQA_HEADS + 2 * GQA_KV_HEADS) * HEAD_DIM
    o_pool = o_cq + MLA_Q_RANK + MLA_KV_RANK + MLA_ROPE
    sw64 = _swap_half(HEAD_DIM)
    sw32 = _swap_half(MLA_ROPE)
    w_in_ord = jnp.concatenate([w_in[:, :, o_cq:o_pool], w_in[:, :, :o_cq], w_in[:, :, o_pool:]], axis=2)
    assert w_in_ord.shape[2] == _IN_EXT
    w_inT = jnp.transpose(w_in_ord, (0, 2, 1)).astype(BF16)
    w_uqT = jnp.transpose(w_uq, (0, 2, 1)).astype(BF16)
    w_ukvT = jnp.transpose(w_ukv, (0, 2, 1)).astype(BF16)

    gq_r, gk_r = g_q_mla[:, MLA_NOPE:], g_k_mla[:, MLA_NOPE:]
    gpack = jnp.concatenate([
        g_q_gqa, g_q_gqa[:, sw64], g_k_gqa, g_k_gqa[:, sw64], g_cq, g_ckv,
        g_q_mla[:, :MLA_NOPE], gq_r, gq_r[:, sw32], g_k_mla[:, :MLA_NOPE], gk_r, gk_r[:, sw32]], axis=1)
    assert gpack.shape[1] == _G_ROWS
    return dict(
        w_modT=jnp.transpose(w_mod, (0, 2, 1)).astype(BF16),
        b_rep=_lane_rep(b_mod),
        gn1=_lane_rep(g_norm1), gn2=_lane_rep(g_norm2),
        w_inT=w_inT, w_uqT=w_uqT, w_ukvT=w_ukvT, gpack=_lane_rep(gpack),
        w_poolT=jnp.transpose(w_pool, (0, 1, 3, 2)).astype(BF16),
        ls_rep=_lane_rep(ls_pool),
        w_outT=jnp.transpose(w_out, (0, 2, 1)).astype(BF16),
        w1T=jnp.transpose(w_mlp1, (0, 2, 1)).astype(BF16),
        w2T=jnp.transpose(w_mlp2, (0, 2, 1)).astype(BF16),
    )


def kernel(x, c, ctx, c_ctx, w_mod, b_mod, g_norm1, g_norm2, w_in, g_q_gqa, g_k_gqa, g_cq, g_ckv, w_uq, w_ukv, g_q_mla, g_k_mla, w_pool, ls_pool, w_out, w_mlp1, w_mlp2):
    batch, seq, d_model = x.shape
    ctx_len = ctx.shape[1]
    depth = w_in.shape[0]
    assert ctx_len % TOKEN_TILE == 0 and seq % TOKEN_TILE == 0 and seq % GRID_W == 0
    assert TOKEN_TILE == Q_TILE
    assert ctx_len % KV_CHUNK_ONLINE == 0 and seq % KV_CHUNK_ONLINE == 0
    assert GQA_HEADS % HEADS_PER_STEP == 0 and HEADS_PER_STEP % GQA_GROUP == 0
    assert MLA_HEADS % HEADS_PER_STEP == 0
    n_ctx_tiles = ctx_len // TOKEN_TILE

    def bounded_flag(g_q, g_k, dim):
        bound = dim ** 0.5 * LOG2E * jnp.max(jnp.abs(g_q), axis=1) * jnp.max(jnp.abs(g_k), axis=1)
        return (bound <= SAFE_LOG2_BOUND).astype(jnp.int32)

    flag_g = bounded_flag(g_q_gqa, g_k_gqa, HEAD_DIM)
    flag_m = bounded_flag(g_q_mla, g_k_mla, MLA_QK)

    p = _prep_weights(w_mod, b_mod, g_norm1, g_norm2, w_in, g_q_gqa, g_k_gqa, g_cq, g_ckv, w_uq,
                      w_ukv, g_q_mla, g_k_mla, w_pool, ls_pool, w_out, w_mlp1, w_mlp2)
    ropeT = _rope_table(seq, ctx_len)
    c_rep = _lane_rep(jnp.concatenate([c, c_ctx[None, :]], axis=0))
    modT = _mod_call(p["w_modT"], c_rep, p["b_rep"])

    xT = _to_feature_major_call(ctx, x, n_ctx_tiles)
    for i in range(depth):
        qg, kg, vg, qm, km, vm, pool_in = _proj_call(
            i, xT, modT, p["gn1"], p["w_inT"], p["gpack"], ropeT, p["w_uqT"], p["w_ukvT"], n_ctx_tiles)
        ogT = _attn_call(flag_g[i:i + 1], qg, kg, vg, GQA_GROUP, True, n_ctx_tiles, ctx_len,
                         "attn_gqa")
        omT = _attn_call(flag_m[i:i + 1], qm, km, vm, 1, False, n_ctx_tiles, ctx_len, "attn_mla")
        xT = _mix_mlp_call(i, xT, ogT, omT, pool_in, modT, p["gn2"], p["ls_rep"], p["w_poolT"],
                           p["w_outT"], p["w1T"], p["w2T"], n_ctx_tiles, final=i == depth - 1)
    return xT
```

```python
import functools
import math

import numpy as np
import jax
import jax.numpy as jnp
from jax import lax
from jax.experimental import pallas as pl
from jax.experimental.pallas import tpu as pltpu

F32 = jnp.float32
BF16 = jnp.bfloat16

GRID_W = 64
ROPE_THETA = 10000.0
NORM_EPS = 1e-6
GQA_HEADS = 6
GQA_KV_HEADS = 2
GQA_GROUP = GQA_HEADS // GQA_KV_HEADS
HEAD_DIM = 64
MLA_HEADS = 6
MLA_NOPE = 64
MLA_ROPE = 32
MLA_QK = MLA_NOPE + MLA_ROPE
MLA_V = 64
MLA_Q_RANK = 256
MLA_KV_RANK = 128
POOL_WINDOWS = (2, 4, 8, 16)
POOL_GROUP_DIM = 64
POOL_WIDTH = len(POOL_WINDOWS) * POOL_GROUP_DIM
N_MOD = 6

LANES = 128
QK_PAD = 128
TOKEN_TILE = 256
Q_TILE = 256
KV_CHUNK_MAX = 8448
KV_SUB = 256
CHAIN_SKEW = 5
KV_CHUNK_ONLINE = 256
HEADS_PER_STEP = 6
V_ROWS = HEAD_DIM
SUBLANES = 8
FF_CHUNK = 256
VMEM_LIMIT = 56 * 1024 * 1024
LOG2E = 1.4426950408889634
SAFE_LOG2_BOUND = 64.0

_L_CQ = 0
_L_CKV = _L_CQ + MLA_Q_RANK
_L_KR = _L_CKV + MLA_KV_RANK
_N_LAT = _L_KR + MLA_ROPE
_O_Q = 0
_O_K = _O_Q + GQA_HEADS * HEAD_DIM
_O_V = _O_K + GQA_KV_HEADS * HEAD_DIM
_O_POOL = _O_V + GQA_KV_HEADS * HEAD_DIM
_IN_EXT = _N_LAT + _O_POOL + POOL_WIDTH

_G_Q = 0
_G_QS = _G_Q + HEAD_DIM
_G_K = _G_QS + HEAD_DIM
_G_KS = _G_K + HEAD_DIM
_G_CQ = _G_KS + HEAD_DIM
_G_CKV = _G_CQ + MLA_Q_RANK
_G_QM_N = _G_CKV + MLA_KV_RANK
_G_QM_R = _G_QM_N + MLA_NOPE
_G_QM_RS = _G_QM_R + MLA_ROPE
_G_KM_N = _G_QM_RS + MLA_ROPE
_G_KM_R = _G_KM_N + MLA_NOPE
_G_KM_RS = _G_KM_R + MLA_ROPE
_G_ROWS = _G_KM_RS + MLA_ROPE

_R_CG = 0
_R_SG = _R_CG + HEAD_DIM
_R_CM = _R_SG + HEAD_DIM
_R_SM = _R_CM + MLA_ROPE
_R_ROWS = _R_SM + MLA_ROPE


def _wide(a, width):
    return jnp.concatenate([a] * (width // LANES), axis=1)


def _dot(a, b):
    return jnp.dot(a, b, preferred_element_type=F32)


def _to_feature_major_kernel(ctx_ref, x_ref, o_ref, *, n_ctx_tiles):
    is_ctx = pl.program_id(1) < n_ctx_tiles

    @pl.when(is_ctx)
    def _():
        o_ref[0] = ctx_ref[0].T

    @pl.when(jnp.logical_not(is_ctx))
    def _():
        o_ref[0] = x_ref[0].T


def _to_feature_major_call(ctx, x, n_ctx_tiles):
    b, seq, d = x.shape
    tm = TOKEN_TILE
    nt = n_ctx_tiles + seq // tm
    return pl.pallas_call(
        functools.partial(_to_feature_major_kernel, n_ctx_tiles=n_ctx_tiles),
        grid=(b, nt),
        in_specs=[
            pl.BlockSpec((1, tm, d), lambda bi, ti: (bi, jnp.minimum(ti, n_ctx_tiles - 1), 0)),
            pl.BlockSpec((1, tm, d), lambda bi, ti: (bi, jnp.maximum(ti - n_ctx_tiles, 0), 0)),
        ],
        out_specs=pl.BlockSpec((1, d, tm), lambda bi, ti: (bi, 0, ti)),
        out_shape=jax.ShapeDtypeStruct((b, d, nt * tm), F32),
        compiler_params=pltpu.CompilerParams(
            dimension_semantics=("arbitrary", "arbitrary"), vmem_limit_bytes=VMEM_LIMIT),
        name="to_feature_major",
    )(ctx, x)


def _mod_kernel(w_ref, c_ref, b_ref, o_ref):
    w = w_ref[0]
    b = b_ref[0]
    for j in range(c_ref.shape[0]):
        c = c_ref[j]
        s = c / (1.0 + jnp.exp(-c))
        o_ref[0, j, 0] = _dot(w, s.astype(BF16)) + b


def _mod_call(w_modT, c_rep, b_rep):
    depth, six_d, d = w_modT.shape
    nvec = c_rep.shape[0]
    return pl.pallas_call(
        _mod_kernel,
        grid=(depth, N_MOD),
        in_specs=[
            pl.BlockSpec((1, d, d), lambda i, n: (i, n, 0)),
            pl.BlockSpec((nvec, d, LANES), lambda i, n: (0, 0, 0)),
            pl.BlockSpec((1, d, LANES), lambda i, n: (i, n, 0)),
        ],
        out_specs=pl.BlockSpec((1, nvec, 1, d, LANES), lambda i, n: (i, 0, n, 0, 0)),
        out_shape=jax.ShapeDtypeStruct((depth, nvec, N_MOD, d, LANES), F32),
        compiler_params=pltpu.CompilerParams(
            dimension_semantics=("arbitrary", "arbitrary"), vmem_limit_bytes=VMEM_LIMIT),
        name="mod_vectors",
    )(w_modT, c_rep, b_rep)


def _swap_halves(a):
    half = a.shape[0] // 2
    return jnp.concatenate([a[half:], a[:half]], axis=0)


def _rms_rows(a, n):
    return lax.rsqrt(jnp.sum(a * a, axis=0, keepdims=True) * (1.0 / n) + NORM_EPS)


def _proj_kernel(x_ref, *refs):
    _proj_body(x_ref[0], *refs)


def _proj_body(x, sh_ref, sc_ref, gn_ref, w_ref, gp_ref, rope_ref, wuq_ref, wukv_ref,
               qg_ref, kg_ref, vg_ref, qm_ref, km_ref, vm_ref, pool_ref):
    d_model, tm = x.shape
    wide = functools.partial(_wide, width=tm)
    xn = x * _rms_rows(x, d_model) * wide(gn_ref[0])
    h = xn * (1.0 + wide(sc_ref[0, 0, 0])) + wide(sh_ref[0, 0, 0])
    hb = h.astype(BF16)
    gp = gp_ref[0]

    def gain(off, n):
        return wide(gp[off:off + n])

    u_lat = _dot(w_ref[0, :_N_LAT, :], hb)
    u = _dot(w_ref[0, _N_LAT:, :], hb)
    cq = u_lat[_L_CQ:_L_CQ + MLA_Q_RANK]
    cq_n = cq * _rms_rows(cq, MLA_Q_RANK) * gain(_G_CQ, MLA_Q_RANK)
    qm_all = _dot(wuq_ref[0], cq_n.astype(BF16))
    ckv = u_lat[_L_CKV:_L_CKV + MLA_KV_RANK]
    ckv_n = ckv * _rms_rows(ckv, MLA_KV_RANK) * gain(_G_CKV, MLA_KV_RANK)
    kv_all = _dot(wukv_ref[0], ckv_n.astype(BF16))
    kr = u_lat[_L_KR:_L_KR + MLA_ROPE]
    kr_sw = _swap_halves(kr)
    kr_ss = jnp.sum(kr * kr, axis=0, keepdims=True)

    cg = rope_ref[_R_CG:_R_CG + HEAD_DIM]
    sg = rope_ref[_R_SG:_R_SG + HEAD_DIM]
    cm = rope_ref[_R_CM:_R_CM + MLA_ROPE]
    sm = rope_ref[_R_SM:_R_SM + MLA_ROPE]
    zeros_kv = jnp.zeros((HEAD_DIM, tm), F32)
    zeros_pad = jnp.zeros((QK_PAD - MLA_QK, tm), F32)

    scale_g = HEAD_DIM ** -0.5 * LOG2E
    gcq = gain(_G_Q, HEAD_DIM) * cg * scale_g
    gsq = gain(_G_QS, HEAD_DIM) * sg * scale_g
    for hd in range(GQA_HEADS):
        a = u[_O_Q + hd * HEAD_DIM:_O_Q + (hd + 1) * HEAD_DIM]
        q = _rms_rows(a, HEAD_DIM) * (a * gcq + _swap_halves(a) * gsq)
        parts = [zeros_kv] * GQA_KV_HEADS
        parts[hd // GQA_GROUP] = q
        qg_ref[0, hd] = jnp.concatenate(parts, axis=0).astype(BF16)
    gck = gain(_G_K, HEAD_DIM) * cg
    gsk = gain(_G_KS, HEAD_DIM) * sg
    ks = []
    for hd in range(GQA_KV_HEADS):
        a = u[_O_K + hd * HEAD_DIM:_O_K + (hd + 1) * HEAD_DIM]
        ks.append(_rms_rows(a, HEAD_DIM) * (a * gck + _swap_halves(a) * gsk))
    kg_ref[0, 0] = jnp.concatenate(ks, axis=0).T.astype(BF16)
    for hd in range(GQA_KV_HEADS):
        vg_ref[0, hd] = u[_O_V + hd * HEAD_DIM:_O_V + (hd + 1) * HEAD_DIM].astype(BF16)

    scale_m = MLA_QK ** -0.5 * LOG2E
    gq_n = gain(_G_QM_N, MLA_NOPE) * scale_m
    gq_c = gain(_G_QM_R, MLA_ROPE) * cm * scale_m
    gq_s = gain(_G_QM_RS, MLA_ROPE) * sm * scale_m
    gk_n = gain(_G_KM_N, MLA_NOPE)
    gk_c = gain(_G_KM_R, MLA_ROPE) * cm
    gk_s = gain(_G_KM_RS, MLA_ROPE) * sm
    k_rope = kr * gk_c + kr_sw * gk_s
    for hd in range(MLA_HEADS):
        a = qm_all[hd * MLA_QK:(hd + 1) * MLA_QK]
        r = _rms_rows(a, MLA_QK)
        a_n, a_r = a[:MLA_NOPE], a[MLA_NOPE:]
        q = jnp.concatenate(
            [r * (a_n * gq_n), r * (a_r * gq_c + _swap_halves(a_r) * gq_s), zeros_pad], axis=0)
        qm_ref[0, hd] = q.astype(BF16)

        kvh = kv_all[hd * (MLA_NOPE + MLA_V):(hd + 1) * (MLA_NOPE + MLA_V)]
        kn, v = kvh[:MLA_NOPE], kvh[MLA_NOPE:]
        rk = lax.rsqrt((jnp.sum(kn * kn, axis=0, keepdims=True) + kr_ss) * (1.0 / MLA_QK) + NORM_EPS)
        k = jnp.concatenate([rk * (kn * gk_n), rk * k_rope, zeros_pad], axis=0)
        km_ref[0, hd] = k.T.astype(BF16)
        vm_ref[0, hd] = v.astype(BF16)

    pool_ref[0] = u[_O_POOL:_O_POOL + POOL_WIDTH]


def _proj_io(layer, b, d, t, modT, gn1, w_inT, gpack, ropeT, w_uqT, w_ukvT, n_ctx_tiles):
    tm = TOKEN_TILE
    nb = modT.shape[1] - 1

    def mod_spec(slot):
        return pl.BlockSpec((1, 1, 1, d, LANES),
                            lambda bi, ti: (layer, jnp.where(ti < n_ctx_tiles, nb, bi), slot, 0, 0))

    def const_spec(arr):
        return pl.BlockSpec((1,) + arr.shape[1:], lambda bi, ti: (layer,) + (0,) * (arr.ndim - 1))

    out_shapes = (
        jax.ShapeDtypeStruct((b, GQA_HEADS, QK_PAD, t), BF16),
        jax.ShapeDtypeStruct((b, 1, t, QK_PAD), BF16),
        jax.ShapeDtypeStruct((b, GQA_KV_HEADS, V_ROWS, t), BF16),
        jax.ShapeDtypeStruct((b, MLA_HEADS, QK_PAD, t), BF16),
        jax.ShapeDtypeStruct((b, MLA_HEADS, t, QK_PAD), BF16),
        jax.ShapeDtypeStruct((b, MLA_HEADS, V_ROWS, t), BF16),
        jax.ShapeDtypeStruct((b, POOL_WIDTH, t), F32),
    )
    out_specs = (
        pl.BlockSpec((1, GQA_HEADS, QK_PAD, tm), lambda bi, ti: (bi, 0, 0, ti)),
        pl.BlockSpec((1, 1, tm, QK_PAD), lambda bi, ti: (bi, 0, ti, 0)),
        pl.BlockSpec((1, GQA_KV_HEADS, V_ROWS, tm), lambda bi, ti: (bi, 0, 0, ti)),
        pl.BlockSpec((1, MLA_HEADS, QK_PAD, tm), lambda bi, ti: (bi, 0, 0, ti)),
        pl.BlockSpec((1, MLA_HEADS, tm, QK_PAD), lambda bi, ti: (bi, 0, ti, 0)),
        pl.BlockSpec((1, MLA_HEADS, V_ROWS, tm), lambda bi, ti: (bi, 0, 0, ti)),
        pl.BlockSpec((1, POOL_WIDTH, tm), lambda bi, ti: (bi, 0, ti)),
    )
    in_specs = [
        mod_spec(0), mod_spec(1),
        const_spec(gn1), const_spec(w_inT), const_spec(gpack),
        pl.BlockSpec((_R_ROWS, tm), lambda bi, ti: (0, ti)),
        const_spec(w_uqT), const_spec(w_ukvT),
    ]
    operands = (modT, modT, gn1, w_inT, gpack, ropeT, w_uqT, w_ukvT)
    return in_specs, operands, out_specs, out_shapes


def _proj_call(layer, xT, modT, gn1, w_inT, gpack, ropeT, w_uqT, w_ukvT, n_ctx_tiles):
    b, d, t = xT.shape
    tm = TOKEN_TILE
    in_specs, operands, out_specs, out_shapes = _proj_io(
        layer, b, d, t, modT, gn1, w_inT, gpack, ropeT, w_uqT, w_ukvT, n_ctx_tiles)
    return pl.pallas_call(
        _proj_kernel,
        grid=(b, t // tm),
        in_specs=[pl.BlockSpec((1, d, tm), lambda bi, ti: (bi, 0, ti))] + in_specs,
        out_specs=out_specs,
        out_shape=out_shapes,
        compiler_params=pltpu.CompilerParams(
            dimension_semantics=("arbitrary", "arbitrary"), vmem_limit_bytes=VMEM_LIMIT),
        name="proj",
    )(xT, *operands)


def _attn_kernel(flag_ref, q_ref, k_ref, v_ref, o_ref, acc_ref, l_ref, *, q_per_kv, shared_k,
                 n_ctx_tiles, ctx_len, seq_len, kv_chunk):
    heads = q_ref.shape[1]
    tq = q_ref.shape[3]
    is_ctx = pl.program_id(2) < n_ctx_tiles

    def k_of(g):
        return 0 if shared_k else g // q_per_kv

    def v_of(g):
        return g // q_per_kv

    bounded = flag_ref[0] != 0

    def block(start, size, first):
        started = set()

        def finish_chain(g, lo, s):
            p = jnp.exp2(s)
            p_sum = jnp.sum(p.reshape(KV_SUB // SUBLANES, SUBLANES, tq), axis=0)
            pv = _dot(v_ref[0, v_of(g), :, pl.ds(lo, KV_SUB)], p.astype(BF16))
            if first and g not in started:
                acc_ref[g] = pv
                l_ref[g] = p_sum
            else:
                acc_ref[g] += pv
                l_ref[g] += p_sum
            started.add(g)

        pending = []
        for c in range(size // KV_SUB):
            lo = start + c * KV_SUB
            for g in range(heads):
                s = _dot(k_ref[0, k_of(g), pl.ds(lo, KV_SUB), :], q_ref[0, g])
                pending.append((g, lo, s))
                if len(pending) > CHAIN_SKEW:
                    finish_chain(*pending.pop(0))
        for chain in pending:
            finish_chain(*chain)

    @pl.when(jnp.logical_and(bounded, is_ctx))
    def _bounded_scores_ctx():
        block(0, ctx_len, True)

    @pl.when(jnp.logical_and(bounded, jnp.logical_not(is_ctx)))
    def _bounded_scores_latent():
        block(0, kv_chunk, True)

        def step(j, carry):
            block(pl.multiple_of(j * kv_chunk, kv_chunk), kv_chunk, False)
            return carry

        lax.fori_loop(1, (ctx_len + seq_len) // kv_chunk, step, 0)

    @pl.when(jnp.logical_not(bounded))
    def _online_softmax():
        chunk = KV_CHUNK_ONLINE
        n = jnp.where(is_ctx, ctx_len // chunk, (ctx_len + seq_len) // chunk)
        for g in range(heads):
            q = q_ref[0, g]

            def step(j, carry, g=g, q=q):
                m, l, acc = carry
                start = pl.multiple_of(j * chunk, chunk)
                s = _dot(k_ref[0, k_of(g), pl.ds(start, chunk), :], q)
                m_new = jnp.maximum(m, jnp.max(s, axis=0, keepdims=True))
                alpha = jnp.exp2(m - m_new)
                p = jnp.exp2(s - m_new)
                pv = _dot(v_ref[0, v_of(g), :, pl.ds(start, chunk)], p.astype(BF16))
                return m_new, alpha * l + jnp.sum(p, axis=0, keepdims=True), alpha * acc + pv

            init = (jnp.full((1, tq), -jnp.inf, F32), jnp.zeros((1, tq), F32),
                    jnp.zeros((V_ROWS, tq), F32))
            _, l, acc = lax.fori_loop(0, n, step, init)
            acc_ref[g] = acc
            l_ref[g] = jnp.concatenate([l, jnp.zeros((SUBLANES - 1, tq), F32)], axis=0)

    for g in range(heads):
        l = jnp.sum(l_ref[g], axis=0, keepdims=True)
        o_ref[0, g * HEAD_DIM:(g + 1) * HEAD_DIM] = (acc_ref[g] / l).astype(BF16)


def _kv_chunk(t):
    return max(c for c in range(Q_TILE, min(t, KV_CHUNK_MAX) + 1, Q_TILE) if t % c == 0)


def _attn_call(flag, qT, k, vT, q_per_kv, shared_k, n_ctx_tiles, ctx_len, name):
    b, nh, _, t = qT.shape
    tq = Q_TILE
    g = HEADS_PER_STEP
    nv = g // q_per_kv
    nk = 1 if shared_k else nv
    kern = functools.partial(_attn_kernel, q_per_kv=q_per_kv, shared_k=shared_k,
                             n_ctx_tiles=n_ctx_tiles, ctx_len=ctx_len, seq_len=t - ctx_len,
                             kv_chunk=_kv_chunk(t))
    k_map = ((lambda bi, gi, qi, f: (bi, 0, 0, 0)) if shared_k
             else (lambda bi, gi, qi, f: (bi, gi, 0, 0)))
    grid_spec = pltpu.PrefetchScalarGridSpec(
        num_scalar_prefetch=1,
        grid=(b, nh // g, t // tq),
        in_specs=[
            pl.BlockSpec((1, g, QK_PAD, tq), lambda bi, gi, qi, f: (bi, gi, 0, qi)),
            pl.BlockSpec((1, nk, t, QK_PAD), k_map),
            pl.BlockSpec((1, nv, V_ROWS, t), lambda bi, gi, qi, f: (bi, gi, 0, 0)),
        ],
        out_specs=pl.BlockSpec((1, g * HEAD_DIM, tq), lambda bi, gi, qi, f: (bi, gi, qi)),
        scratch_shapes=[pltpu.VMEM((g, V_ROWS, tq), F32), pltpu.VMEM((g, SUBLANES, tq), F32)],
    )
    return pl.pallas_call(
        kern,
        grid_spec=grid_spec,
        out_shape=jax.ShapeDtypeStruct((b, nh * HEAD_DIM, t), BF16),
        compiler_params=pltpu.CompilerParams(
            dimension_semantics=("arbitrary", "arbitrary", "arbitrary"),
            vmem_limit_bytes=VMEM_LIMIT),
        name=name,
    )(flag, qT, k, vT)


_N_MIX_INPUTS = 16
_N_PROJ_INPUTS = 8


def _mix_mlp_kernel(*refs, n_ctx_tiles, n_tiles, token_major_out, fuse_next):
    (x_ref, og_ref, om_ref, pp_ref, pc_ref, pn_ref, gt1_ref, sh2_ref, sc2_ref,
     gt2_ref, gn_ref, ls_ref, wp_ref, wo_ref, w1_ref, w2_ref) = refs[:_N_MIX_INPUTS]
    n_in = _N_MIX_INPUTS + (_N_PROJ_INPUTS if fuse_next else 0)
    o_ref = refs[n_in]
    tm = x_ref.shape[2]
    wide = functools.partial(_wide, width=tm)
    ti = pl.program_id(1)

    n_og = og_ref.shape[1]
    n_om = om_ref.shape[1]
    half = wo_ref.shape[1] // 2
    y_lo = (_dot(wo_ref[0, :half, :n_og], og_ref[0])
            + _dot(wo_ref[0, :half, n_og:n_og + n_om], om_ref[0]))
    y_hi = (_dot(wo_ref[0, half:, :n_og], og_ref[0])
            + _dot(wo_ref[0, half:, n_og:n_og + n_om], om_ref[0]))

    cur = pc_ref[0]
    prev_ok = jnp.logical_and(ti != 0, ti != n_ctx_tiles)
    next_ok = jnp.logical_and(ti != n_ctx_tiles - 1, ti != n_tiles - 1)
    prev = jnp.where(prev_ok, pp_ref[0][:, tm - LANES:], 0.0)
    nxt = jnp.where(next_ok, pn_ref[0][:, :LANES], 0.0)
    ext = jnp.concatenate([prev, cur, nxt], axis=1)
    width = tm + 2 * LANES

    def shifted(a, s):
        return pltpu.roll(a, s % width, axis=1)

    in_ctx = ti < n_ctx_tiles
    seg_start = jnp.where(in_ctx, 0, n_ctx_tiles * tm)
    seg_len = jnp.where(in_ctx, n_ctx_tiles * tm, (n_tiles - n_ctx_tiles) * tm)
    pos = lax.broadcasted_iota(jnp.int32, (1, tm), 1) + (ti * tm - seg_start)
    gd = POOL_GROUP_DIM
    run = ext + shifted(ext, 1)
    outs = []
    for gi, w in enumerate(POOL_WINDOWS):
        if gi > 0:
            quarter = w // 4
            run = run[gd:]
            run = shifted(run, quarter) + shifted(run, -quarter)
        cnt = jnp.minimum(pos + w // 2, seg_len) - jnp.maximum(pos - w // 2, 0)
        mean = run[:gd, LANES:LANES + tm] / cnt.astype(F32)
        dlt = (mean - cur[gi * gd:(gi + 1) * gd]).astype(BF16)
        outs.append(_dot(wp_ref[0, gi], dlt))
    op = jnp.concatenate(outs, axis=0) * wide(ls_ref[0])

    op = op.astype(BF16)
    y = jnp.concatenate([y_lo + _dot(wo_ref[0, :half, n_og + n_om:], op),
                         y_hi + _dot(wo_ref[0, half:, n_og + n_om:], op)], axis=0)
    x1 =x_ref[0] + wide(gt1_ref[0, 0, 0]) * y

    d_model = x1.shape[0]
    xn = x1 * _rms_rows(x1, d_model) * wide(gn_ref[0])
    h2 = (xn * (1.0 + wide(sc2_ref[0, 0, 0])) + wide(sh2_ref[0, 0, 0])).astype(BF16)
    d_ff = w1_ref.shape[1]
    n_chunks = d_ff // FF_CHUNK

    def up(c):
        a = jnp.maximum(_dot(w1_ref[0, c * FF_CHUNK:(c + 1) * FF_CHUNK, :], h2), 0.0)
        return (a * a).astype(BF16)

    y2 = None
    hidden = up(0)
    for c in range(n_chunks):
        nxt = up(c + 1) if c + 1 < n_chunks else None
        part = _dot(w2_ref[0, :, c * FF_CHUNK:(c + 1) * FF_CHUNK], hidden)
        y2 = part if y2 is None else y2 + part
        hidden = nxt
    x2 = x1 + wide(gt2_ref[0, 0, 0]) * y2
    o_ref[0] = x2.T if token_major_out else x2
    if fuse_next:
        _proj_body(x2, *refs[_N_MIX_INPUTS:n_in], *refs[n_in + 1:])


def _mix_mlp_call(layer, xT, ogT, omT, poolT, modT, gn2, ls_rep, w_poolT, w_outT, w1T, w2T,
                  n_ctx_tiles, final, next_proj=None):
    b, d, t = xT.shape
    tm = TOKEN_TILE
    nt = t // tm
    nb = modT.shape[1] - 1

    def mod_spec(slot):
        return pl.BlockSpec((1, 1, 1, d, LANES),
                            lambda bi, ti: (layer, jnp.where(ti < n_ctx_tiles, nb, bi), slot, 0, 0))

    def const_spec(arr):
        return pl.BlockSpec((1,) + arr.shape[1:], lambda bi, ti: (layer,) + (0,) * (arr.ndim - 1))

    def tok_spec(rows):
        return pl.BlockSpec((1, rows, tm), lambda bi, ti: (bi, 0, ti))

    kern = functools.partial(_mix_mlp_kernel, n_ctx_tiles=n_ctx_tiles, n_tiles=nt,
                             token_major_out=final, fuse_next=next_proj is not None)
    p_in_specs, p_operands, p_out_specs, p_out_shapes = next_proj or ([], (), (), ())
    if final:
        out_spec = pl.BlockSpec((1, tm, d), lambda bi, ti: (bi, jnp.maximum(ti - n_ctx_tiles, 0), 0))
        out_shape = jax.ShapeDtypeStruct((b, t - n_ctx_tiles * tm, d), F32)
    else:
        out_spec = tok_spec(d)
        out_shape = jax.ShapeDtypeStruct((b, d, t), F32)
    return pl.pallas_call(
        kern,
        grid=(b, nt),
        in_specs=[
            tok_spec(d), tok_spec(ogT.shape[1]), tok_spec(omT.shape[1]),
            pl.BlockSpec((1, POOL_WIDTH, tm), lambda bi, ti: (bi, 0, jnp.maximum(ti - 1, 0))),
            tok_spec(POOL_WIDTH),
            pl.BlockSpec((1, POOL_WIDTH, tm), lambda bi, ti: (bi, 0, jnp.minimum(ti + 1, nt - 1))),
            mod_spec(2), mod_spec(3), mod_spec(4), mod_spec(5),
            const_spec(gn2), const_spec(ls_rep), const_spec(w_poolT), const_spec(w_outT),
            const_spec(w1T), const_spec(w2T),
        ] + list(p_in_specs),
        out_specs=(out_spec,) + tuple(p_out_specs),
        out_shape=(out_shape,) + tuple(p_out_shapes),
        compiler_params=pltpu.CompilerParams(
            dimension_semantics=("arbitrary", "arbitrary"), vmem_limit_bytes=VMEM_LIMIT),
        name="mix_mlp_proj" if next_proj else "mix_mlp",
    )(xT, ogT, omT, poolT, poolT, poolT, modT, modT, modT, modT, gn2, ls_rep, w_poolT, w_outT,
      w1T, w2T, *p_operands)


def _rope_table(seq_len, ctx_len):
    rows = seq_len // GRID_W
    row = jnp.repeat(jnp.arange(rows, dtype=F32), GRID_W)
    col = jnp.tile(jnp.arange(GRID_W, dtype=F32), rows)

    def block(rot_dim):
        n = rot_dim // 4
        inv = ROPE_THETA ** (-jnp.arange(n, dtype=F32) / n)
        ang = jnp.concatenate([row[:, None] * inv, col[:, None] * inv], axis=-1)
        cos, sin = jnp.cos(ang).T, jnp.sin(ang).T
        c = jnp.concatenate([cos, cos], axis=0)
        s = jnp.concatenate([-sin, sin], axis=0)
        c = jnp.concatenate([jnp.ones((rot_dim, ctx_len), F32), c], axis=1)
        s = jnp.concatenate([jnp.zeros((rot_dim, ctx_len), F32), s], axis=1)
        return c, s

    cg, sg = block(HEAD_DIM)
    cm, sm = block(MLA_ROPE)
    return jnp.concatenate([cg, sg, cm, sm], axis=0)


def _lane_rep(a):
    return jnp.broadcast_to(a[..., None], a.shape + (LANES,)).astype(F32)


def _swap_half(n):
    return (np.arange(n) + n // 2) % n


def _prep_weights(w_mod, b_mod, g_norm1, g_norm2, w_in, g_q_gqa, g_k_gqa, g_cq, g_ckv, w_uq, w_ukv,
                  g_q_mla, g_k_mla, w_pool, ls_pool, w_out, w_mlp1, w_mlp2):
    o_cq = (GQA_HEADS + 2 * GQA_KV_HEADS) * HEAD_DIM
    o_pool = o_cq + MLA_Q_RANK + MLA_KV_RANK + MLA_ROPE
    sw64 = _swap_half(HEAD_DIM)
    sw32 = _swap_half(MLA_ROPE)
    w_in_ord = jnp.concatenate([w_in[:, :, o_cq:o_pool], w_in[:, :, :o_cq], w_in[:, :, o_pool:]], axis=2)
    assert w_in_ord.shape[2] == _IN_EXT
    w_inT = jnp.transpose(w_in_ord, (0, 2, 1)).astype(BF16)
    w_uqT = jnp.transpose(w_uq, (0, 2, 1)).astype(BF16)
    w_ukvT = jnp.transpose(w_ukv, (0, 2, 1)).astype(BF16)

    gq_r, gk_r = g_q_mla[:, MLA_NOPE:], g_k_mla[:, MLA_NOPE:]
    gpack = jnp.concatenate([
        g_q_gqa, g_q_gqa[:, sw64], g_k_gqa, g_k_gqa[:, sw64], g_cq, g_ckv,
        g_q_mla[:, :MLA_NOPE], gq_r, gq_r[:, sw32], g_k_mla[:, :MLA_NOPE], gk_r, gk_r[:, sw32]], axis=1)
    assert gpack.shape[1] == _G_ROWS
    return dict(
        w_modT=jnp.transpose(w_mod, (0, 2, 1)).astype(BF16),
        b_rep=_lane_rep(b_mod),
        gn1=_lane_rep(g_norm1), gn2=_lane_rep(g_norm2),
        w_inT=w_inT, w_uqT=w_uqT, w_ukvT=w_ukvT, gpack=_lane_rep(gpack),
        w_poolT=jnp.transpose(w_pool, (0, 1, 3, 2)).astype(BF16),
        ls_rep=_lane_rep(ls_pool),
        w_outT=jnp.transpose(w_out, (0, 2, 1)).astype(BF16),
        w1T=jnp.transpose(w_mlp1, (0, 2, 1)).astype(BF16),
        w2T=jnp.transpose(w_mlp2, (0, 2, 1)).astype(BF16),
    )


def kernel(x, c, ctx, c_ctx, w_mod, b_mod, g_norm1, g_norm2, w_in, g_q_gqa, g_k_gqa, g_cq, g_ckv, w_uq, w_ukv, g_q_mla, g_k_mla, w_pool, ls_pool, w_out, w_mlp1, w_mlp2):
    batch, seq, d_model = x.shape
    ctx_len = ctx.shape[1]
    depth = w_in.shape[0]
    assert ctx_len % TOKEN_TILE == 0 and seq % TOKEN_TILE == 0 and seq % GRID_W == 0
    assert TOKEN_TILE == Q_TILE
    assert ctx_len % KV_CHUNK_ONLINE == 0 and seq % KV_CHUNK_ONLINE == 0
    assert GQA_HEADS % HEADS_PER_STEP == 0 and HEADS_PER_STEP % GQA_GROUP == 0
    assert MLA_HEADS % HEADS_PER_STEP == 0
    n_ctx_tiles = ctx_len // TOKEN_TILE

    def bounded_flag(g_q, g_k, dim):
        bound = dim ** 0.5 * LOG2E * jnp.max(jnp.abs(g_q), axis=1) * jnp.max(jnp.abs(g_k), axis=1)
        return (bound <= SAFE_LOG2_BOUND).astype(jnp.int32)

    flag_g = bounded_flag(g_q_gqa, g_k_gqa, HEAD_DIM)
    flag_m = bounded_flag(g_q_mla, g_k_mla, MLA_QK)

    p = _prep_weights(w_mod, b_mod, g_norm1, g_norm2, w_in, g_q_gqa, g_k_gqa, g_cq, g_ckv, w_uq,
                      w_ukv, g_q_mla, g_k_mla, w_pool, ls_pool, w_out, w_mlp1, w_mlp2)
    ropeT = _rope_table(seq, ctx_len)
    c_rep = _lane_rep(jnp.concatenate([c, c_ctx[None, :]], axis=0))
    modT = _mod_call(p["w_modT"], c_rep, p["b_rep"])

    xT = _to_feature_major_call(ctx, x, n_ctx_tiles)
    t_all = xT.shape[2]
    proj_weights = (modT, p["gn1"], p["w_inT"], p["gpack"], ropeT, p["w_uqT"], p["w_ukvT"], n_ctx_tiles)
    qg, kg, vg, qm, km, vm, pool_in = _proj_call(0, xT, *proj_weights)
    for i in range(depth):
        last = i == depth - 1
        ogT = _attn_call(flag_g[i:i + 1], qg, kg, vg, GQA_GROUP, True, n_ctx_tiles, ctx_len,
                         "attn_gqa")
        omT = _attn_call(flag_m[i:i + 1], qm, km, vm, 1, False, n_ctx_tiles, ctx_len, "attn_mla")
        next_proj = None if last else _proj_io(i + 1, batch, d_model, t_all, *proj_weights)
        outs = _mix_mlp_call(i, xT, ogT, omT, pool_in, modT, p["gn2"], p["ls_rep"], p["w_poolT"],
                             p["w_outT"], p["w1T"], p["w2T"], n_ctx_tiles, final=last,
                             next_proj=next_proj)
        xT = outs[0]
        if not last:
            qg, kg, vg, qm, km, vm, pool_in = outs[1:]
    return xT
```

```python
import functools
import math

import numpy as np
import jax
import jax.numpy as jnp
from jax import lax
from jax.experimental import pallas as pl
from jax.experimental.pallas import tpu as pltpu

F32 = jnp.float32
BF16 = jnp.bfloat16

GRID_W = 64
ROPE_THETA = 10000.0
NORM_EPS = 1e-6
GQA_HEADS = 6
GQA_KV_HEADS = 2
GQA_GROUP = GQA_HEADS // GQA_KV_HEADS
HEAD_DIM = 64
MLA_HEADS = 6
MLA_NOPE = 64
MLA_ROPE = 32
MLA_QK = MLA_NOPE + MLA_ROPE
MLA_V = 64
MLA_Q_RANK = 256
MLA_KV_RANK = 128
POOL_WINDOWS = (2, 4, 8, 16)
POOL_GROUP_DIM = 64
POOL_WIDTH = len(POOL_WINDOWS) * POOL_GROUP_DIM
N_MOD = 6

LANES = 128
QK_PAD = 128
TOKEN_TILE = 256
Q_TILE = 256
KV_CHUNK_MAX = 8448
KV_SUB = 256
CHAIN_SKEW = 5
KV_CHUNK_ONLINE = 256
HEADS_PER_STEP = 6
V_ROWS = HEAD_DIM
SUBLANES = 8
FF_CHUNK = 256
VMEM_LIMIT = 56 * 1024 * 1024
LOG2E = 1.4426950408889634
SAFE_LOG2_BOUND = 64.0

_L_CQ = 0
_L_CKV = _L_CQ + MLA_Q_RANK
_L_KR = _L_CKV + MLA_KV_RANK
_N_LAT = _L_KR + MLA_ROPE
_O_Q = 0
_O_K = _O_Q + GQA_HEADS * HEAD_DIM
_O_V = _O_K + GQA_KV_HEADS * HEAD_DIM
_O_POOL = _O_V + GQA_KV_HEADS * HEAD_DIM
_IN_EXT = _N_LAT + _O_POOL + POOL_WIDTH

_G_Q = 0
_G_QS = _G_Q + HEAD_DIM
_G_K = _G_QS + HEAD_DIM
_G_KS = _G_K + HEAD_DIM
_G_CQ = _G_KS + HEAD_DIM
_G_CKV = _G_CQ + MLA_Q_RANK
_G_QM_N = _G_CKV + MLA_KV_RANK
_G_QM_R = _G_QM_N + MLA_NOPE
_G_QM_RS = _G_QM_R + MLA_ROPE
_G_KM_N = _G_QM_RS + MLA_ROPE
_G_KM_R = _G_KM_N + MLA_NOPE
_G_KM_RS = _G_KM_R + MLA_ROPE
_G_ROWS = _G_KM_RS + MLA_ROPE

_R_CG = 0
_R_SG = _R_CG + HEAD_DIM
_R_CM = _R_SG + HEAD_DIM
_R_SM = _R_CM + MLA_ROPE
_R_ROWS = _R_SM + MLA_ROPE


def _wide(a, width):
    return jnp.concatenate([a] * (width // LANES), axis=1)


def _dot(a, b):
    return jnp.dot(a, b, preferred_element_type=F32)


def _to_feature_major_kernel(ctx_ref, x_ref, o_ref, *, n_ctx_tiles):
    is_ctx = pl.program_id(1) < n_ctx_tiles

    @pl.when(is_ctx)
    def _():
        o_ref[0] = ctx_ref[0].T

    @pl.when(jnp.logical_not(is_ctx))
    def _():
        o_ref[0] = x_ref[0].T


def _to_feature_major_call(ctx, x, n_ctx_tiles):
    b, seq, d = x.shape
    tm = TOKEN_TILE
    nt = n_ctx_tiles + seq // tm
    return pl.pallas_call(
        functools.partial(_to_feature_major_kernel, n_ctx_tiles=n_ctx_tiles),
        grid=(b, nt),
        in_specs=[
            pl.BlockSpec((1, tm, d), lambda bi, ti: (bi, jnp.minimum(ti, n_ctx_tiles - 1), 0)),
            pl.BlockSpec((1, tm, d), lambda bi, ti: (bi, jnp.maximum(ti - n_ctx_tiles, 0), 0)),
        ],
        out_specs=pl.BlockSpec((1, d, tm), lambda bi, ti: (bi, 0, ti)),
        out_shape=jax.ShapeDtypeStruct((b, d, nt * tm), F32),
        compiler_params=pltpu.CompilerParams(
            dimension_semantics=("arbitrary", "arbitrary"), vmem_limit_bytes=VMEM_LIMIT),
        name="to_feature_major",
    )(ctx, x)


def _mod_kernel(w_ref, c_ref, b_ref, o_ref):
    w = w_ref[0]
    b = b_ref[0]
    nvec, d, _ = c_ref.shape
    lane = lax.broadcasted_iota(jnp.int32, (d, LANES), 1)
    cols = jnp.zeros((d, LANES), F32)
    for j in range(nvec):
        cols = jnp.where(lane == j, c_ref[j], cols)
    s = cols / (1.0 + jnp.exp(-cols))
    res = _dot(w, s.astype(BF16))
    for j in range(nvec):
        o_ref[0, j, 0] = jnp.broadcast_to(res[:, j:j + 1], (d, LANES)) + b


def _mod_call(w_modT, c_rep, b_rep):
    depth, six_d, d = w_modT.shape
    nvec = c_rep.shape[0]
    return pl.pallas_call(
        _mod_kernel,
        grid=(depth, N_MOD),
        in_specs=[
            pl.BlockSpec((1, d, d), lambda i, n: (i, n, 0)),
            pl.BlockSpec((nvec, d, LANES), lambda i, n: (0, 0, 0)),
            pl.BlockSpec((1, d, LANES), lambda i, n: (i, n, 0)),
        ],
        out_specs=pl.BlockSpec((1, nvec, 1, d, LANES), lambda i, n: (i, 0, n, 0, 0)),
        out_shape=jax.ShapeDtypeStruct((depth, nvec, N_MOD, d, LANES), F32),
        compiler_params=pltpu.CompilerParams(
            dimension_semantics=("arbitrary", "arbitrary"), vmem_limit_bytes=VMEM_LIMIT),
        name="mod_vectors",
    )(w_modT, c_rep, b_rep)


def _swap_halves(a):
    half = a.shape[0] // 2
    return jnp.concatenate([a[half:], a[:half]], axis=0)


def _rms_rows(a, n):
    return lax.rsqrt(jnp.sum(a * a, axis=0, keepdims=True) * (1.0 / n) + NORM_EPS)


def _proj_kernel(x_ref, *refs):
    _proj_body(x_ref[0], *refs)


def _proj_body(x, sh_ref, sc_ref, gn_ref, w_ref, gp_ref, rope_ref, wuq_ref, wukv_ref,
               qg_ref, kg_ref, vg_ref, qm_ref, km_ref, vm_ref, pool_ref):
    d_model, tm = x.shape
    wide = functools.partial(_wide, width=tm)
    xn = x * _rms_rows(x, d_model) * wide(gn_ref[0])
    h = xn * (1.0 + wide(sc_ref[0, 0, 0])) + wide(sh_ref[0, 0, 0])
    hb = h.astype(BF16)
    gp = gp_ref[0]

    def gain(off, n):
        return wide(gp[off:off + n])

    u_lat = _dot(w_ref[0, :_N_LAT, :], hb)
    u = _dot(w_ref[0, _N_LAT:, :], hb)
    cq = u_lat[_L_CQ:_L_CQ + MLA_Q_RANK]
    cq_n = cq * _rms_rows(cq, MLA_Q_RANK) * gain(_G_CQ, MLA_Q_RANK)
    qm_all = _dot(wuq_ref[0], cq_n.astype(BF16))
    ckv = u_lat[_L_CKV:_L_CKV + MLA_KV_RANK]
    ckv_n = ckv * _rms_rows(ckv, MLA_KV_RANK) * gain(_G_CKV, MLA_KV_RANK)
    kv_all = _dot(wukv_ref[0], ckv_n.astype(BF16))
    kr = u_lat[_L_KR:_L_KR + MLA_ROPE]
    kr_sw = _swap_halves(kr)
    kr_ss = jnp.sum(kr * kr, axis=0, keepdims=True)

    cg = rope_ref[_R_CG:_R_CG + HEAD_DIM]
    sg = rope_ref[_R_SG:_R_SG + HEAD_DIM]
    cm = rope_ref[_R_CM:_R_CM + MLA_ROPE]
    sm = rope_ref[_R_SM:_R_SM + MLA_ROPE]
    zeros_kv = jnp.zeros((HEAD_DIM, tm), F32)
    zeros_pad = jnp.zeros((QK_PAD - MLA_QK, tm), F32)

    scale_g = HEAD_DIM ** -0.5 * LOG2E
    gcq = gain(_G_Q, HEAD_DIM) * cg * scale_g
    gsq = gain(_G_QS, HEAD_DIM) * sg * scale_g
    for hd in range(GQA_HEADS):
        a = u[_O_Q + hd * HEAD_DIM:_O_Q + (hd + 1) * HEAD_DIM]
        q = _rms_rows(a, HEAD_DIM) * (a * gcq + _swap_halves(a) * gsq)
        parts = [zeros_kv] * GQA_KV_HEADS
        parts[hd // GQA_GROUP] = q
        qg_ref[0, hd] = jnp.concatenate(parts, axis=0).astype(BF16)
    gck = gain(_G_K, HEAD_DIM) * cg
    gsk = gain(_G_KS, HEAD_DIM) * sg
    ks = []
    for hd in range(GQA_KV_HEADS):
        a = u[_O_K + hd * HEAD_DIM:_O_K + (hd + 1) * HEAD_DIM]
        ks.append(_rms_rows(a, HEAD_DIM) * (a * gck + _swap_halves(a) * gsk))
    kg_ref[0, 0] = jnp.concatenate(ks, axis=0).T.astype(BF16)
    for hd in range(GQA_KV_HEADS):
        vg_ref[0, hd] = u[_O_V + hd * HEAD_DIM:_O_V + (hd + 1) * HEAD_DIM].astype(BF16)

    scale_m = MLA_QK ** -0.5 * LOG2E
    gq_n = gain(_G_QM_N, MLA_NOPE) * scale_m
    gq_c = gain(_G_QM_R, MLA_ROPE) * cm * scale_m
    gq_s = gain(_G_QM_RS, MLA_ROPE) * sm * scale_m
    gk_n = gain(_G_KM_N, MLA_NOPE)
    gk_c = gain(_G_KM_R, MLA_ROPE) * cm
    gk_s = gain(_G_KM_RS, MLA_ROPE) * sm
    k_rope = kr * gk_c + kr_sw * gk_s
    for hd in range(MLA_HEADS):
        a = qm_all[hd * MLA_QK:(hd + 1) * MLA_QK]
        r = _rms_rows(a, MLA_QK)
        a_n, a_r = a[:MLA_NOPE], a[MLA_NOPE:]
        q = jnp.concatenate(
            [r * (a_n * gq_n), r * (a_r * gq_c + _swap_halves(a_r) * gq_s), zeros_pad], axis=0)
        qm_ref[0, hd] = q.astype(BF16)

        kvh = kv_all[hd * (MLA_NOPE + MLA_V):(hd + 1) * (MLA_NOPE + MLA_V)]
        kn, v = kvh[:MLA_NOPE], kvh[MLA_NOPE:]
        rk = lax.rsqrt((jnp.sum(kn * kn, axis=0, keepdims=True) + kr_ss) * (1.0 / MLA_QK) + NORM_EPS)
        k = jnp.concatenate([rk * (kn * gk_n), rk * k_rope, zeros_pad], axis=0)
        km_ref[0, hd] = k.T.astype(BF16)
        vm_ref[0, hd] = v.astype(BF16)

    pool_ref[0] = u[_O_POOL:_O_POOL + POOL_WIDTH]


def _proj_io(layer, b, d, t, modT, gn1, w_inT, gpack, ropeT, w_uqT, w_ukvT, n_ctx_tiles):
    tm = TOKEN_TILE
    nb = modT.shape[1] - 1

    def mod_spec(slot):
        return pl.BlockSpec((1, 1, 1, d, LANES),
                            lambda bi, ti: (layer, jnp.where(ti < n_ctx_tiles, nb, bi), slot, 0, 0))

    def const_spec(arr):
        return pl.BlockSpec((1,) + arr.shape[1:], lambda bi, ti: (layer,) + (0,) * (arr.ndim - 1))

    out_shapes = (
        jax.ShapeDtypeStruct((b, GQA_HEADS, QK_PAD, t), BF16),
        jax.ShapeDtypeStruct((b, 1, t, QK_PAD), BF16),
        jax.ShapeDtypeStruct((b, GQA_KV_HEADS, V_ROWS, t), BF16),
        jax.ShapeDtypeStruct((b, MLA_HEADS, QK_PAD, t), BF16),
        jax.ShapeDtypeStruct((b, MLA_HEADS, t, QK_PAD), BF16),
        jax.ShapeDtypeStruct((b, MLA_HEADS, V_ROWS, t), BF16),
        jax.ShapeDtypeStruct((b, POOL_WIDTH, t), F32),
    )
    out_specs = (
        pl.BlockSpec((1, GQA_HEADS, QK_PAD, tm), lambda bi, ti: (bi, 0, 0, ti)),
        pl.BlockSpec((1, 1, tm, QK_PAD), lambda bi, ti: (bi, 0, ti, 0)),
        pl.BlockSpec((1, GQA_KV_HEADS, V_ROWS, tm), lambda bi, ti: (bi, 0, 0, ti)),
        pl.BlockSpec((1, MLA_HEADS, QK_PAD, tm), lambda bi, ti: (bi, 0, 0, ti)),
        pl.BlockSpec((1, MLA_HEADS, tm, QK_PAD), lambda bi, ti: (bi, 0, ti, 0)),
        pl.BlockSpec((1, MLA_HEADS, V_ROWS, tm), lambda bi, ti: (bi, 0, 0, ti)),
        pl.BlockSpec((1, POOL_WIDTH, tm), lambda bi, ti: (bi, 0, ti)),
    )
    in_specs = [
        mod_spec(0), mod_spec(1),
        const_spec(gn1), const_spec(w_inT), const_spec(gpack),
        pl.BlockSpec((_R_ROWS, tm), lambda bi, ti: (0, ti)),
        const_spec(w_uqT), const_spec(w_ukvT),
    ]
    operands = (modT, modT, gn1, w_inT, gpack, ropeT, w_uqT, w_ukvT)
    return in_specs, operands, out_specs, out_shapes


def _proj_call(layer, xT, modT, gn1, w_inT, gpack, ropeT, w_uqT, w_ukvT, n_ctx_tiles):
    b, d, t = xT.shape
    tm = TOKEN_TILE
    in_specs, operands, out_specs, out_shapes = _proj_io(
        layer, b, d, t, modT, gn1, w_inT, gpack, ropeT, w_uqT, w_ukvT, n_ctx_tiles)
    return pl.pallas_call(
        _proj_kernel,
        grid=(b, t // tm),
        in_specs=[pl.BlockSpec((1, d, tm), lambda bi, ti: (bi, 0, ti))] + in_specs,
        out_specs=out_specs,
        out_shape=out_shapes,
        compiler_params=pltpu.CompilerParams(
            dimension_semantics=("arbitrary", "arbitrary"), vmem_limit_bytes=VMEM_LIMIT),
        name="proj",
    )(xT, *operands)


def _attn_kernel(flag_ref, q_ref, k_ref, v_ref, o_ref, acc_ref, l_ref, *, q_per_kv, shared_k,
                 n_ctx_tiles, ctx_len, seq_len, kv_chunk):
    heads = q_ref.shape[1]
    tq = q_ref.shape[3]
    is_ctx = pl.program_id(2) < n_ctx_tiles

    def k_of(g):
        return 0 if shared_k else g // q_per_kv

    def v_of(g):
        return g // q_per_kv

    bounded = flag_ref[0] != 0

    def block(start, size, first):
        started = set()

        def finish_chain(g, lo, s):
            p = jnp.exp2(s)
            p_sum = jnp.sum(p.reshape(KV_SUB // SUBLANES, SUBLANES, tq), axis=0)
            pv = _dot(v_ref[0, v_of(g), :, pl.ds(lo, KV_SUB)], p.astype(BF16))
            if first and g not in started:
                acc_ref[g] = pv
                l_ref[g] = p_sum
            else:
                acc_ref[g] += pv
                l_ref[g] += p_sum
            started.add(g)

        pending = []
        for c in range(size // KV_SUB):
            lo = start + c * KV_SUB
            for g in range(heads):
                s = _dot(k_ref[0, k_of(g), pl.ds(lo, KV_SUB), :], q_ref[0, g])
                pending.append((g, lo, s))
                if len(pending) > CHAIN_SKEW:
                    finish_chain(*pending.pop(0))
        for chain in pending:
            finish_chain(*chain)

    @pl.when(jnp.logical_and(bounded, is_ctx))
    def _bounded_scores_ctx():
        block(0, ctx_len, True)

    @pl.when(jnp.logical_and(bounded, jnp.logical_not(is_ctx)))
    def _bounded_scores_latent():
        block(0, kv_chunk, True)

        def step(j, carry):
            block(pl.multiple_of(j * kv_chunk, kv_chunk), kv_chunk, False)
            return carry

        lax.fori_loop(1, (ctx_len + seq_len) // kv_chunk, step, 0)

    @pl.when(jnp.logical_not(bounded))
    def _online_softmax():
        chunk = KV_CHUNK_ONLINE
        n = jnp.where(is_ctx, ctx_len // chunk, (ctx_len + seq_len) // chunk)
        for g in range(heads):
            q = q_ref[0, g]

            def step(j, carry, g=g, q=q):
                m, l, acc = carry
                start = pl.multiple_of(j * chunk, chunk)
                s = _dot(k_ref[0, k_of(g), pl.ds(start, chunk), :], q)
                m_new = jnp.maximum(m, jnp.max(s, axis=0, keepdims=True))
                alpha = jnp.exp2(m - m_new)
                p = jnp.exp2(s - m_new)
                pv = _dot(v_ref[0, v_of(g), :, pl.ds(start, chunk)], p.astype(BF16))
                return m_new, alpha * l + jnp.sum(p, axis=0, keepdims=True), alpha * acc + pv

            init = (jnp.full((1, tq), -jnp.inf, F32), jnp.zeros((1, tq), F32),
                    jnp.zeros((V_ROWS, tq), F32))
            _, l, acc = lax.fori_loop(0, n, step, init)
            acc_ref[g] = acc
            l_ref[g] = jnp.concatenate([l, jnp.zeros((SUBLANES - 1, tq), F32)], axis=0)

    for g in range(heads):
        l = jnp.sum(l_ref[g], axis=0, keepdims=True)
        o_ref[0, g * HEAD_DIM:(g + 1) * HEAD_DIM] = (acc_ref[g] / l).astype(BF16)


def _kv_chunk(t):
    return max(c for c in range(Q_TILE, min(t, KV_CHUNK_MAX) + 1, Q_TILE) if t % c == 0)


def _attn_call(flag, qT, k, vT, q_per_kv, shared_k, n_ctx_tiles, ctx_len, name):
    b, nh, _, t = qT.shape
    tq = Q_TILE
    g = HEADS_PER_STEP
    nv = g // q_per_kv
    nk = 1 if shared_k else nv
    kern = functools.partial(_attn_kernel, q_per_kv=q_per_kv, shared_k=shared_k,
                             n_ctx_tiles=n_ctx_tiles, ctx_len=ctx_len, seq_len=t - ctx_len,
                             kv_chunk=_kv_chunk(t))
    k_map = ((lambda bi, gi, qi, f: (bi, 0, 0, 0)) if shared_k
             else (lambda bi, gi, qi, f: (bi, gi, 0, 0)))
    grid_spec = pltpu.PrefetchScalarGridSpec(
        num_scalar_prefetch=1,
        grid=(b, nh // g, t // tq),
        in_specs=[
            pl.BlockSpec((1, g, QK_PAD, tq), lambda bi, gi, qi, f: (bi, gi, 0, qi)),
            pl.BlockSpec((1, nk, t, QK_PAD), k_map),
            pl.BlockSpec((1, nv, V_ROWS, t), lambda bi, gi, qi, f: (bi, gi, 0, 0)),
        ],
        out_specs=pl.BlockSpec((1, g * HEAD_DIM, tq), lambda bi, gi, qi, f: (bi, gi, qi)),
        scratch_shapes=[pltpu.VMEM((g, V_ROWS, tq), F32), pltpu.VMEM((g, SUBLANES, tq), F32)],
    )
    return pl.pallas_call(
        kern,
        grid_spec=grid_spec,
        out_shape=jax.ShapeDtypeStruct((b, nh * HEAD_DIM, t), BF16),
        compiler_params=pltpu.CompilerParams(
            dimension_semantics=("arbitrary", "arbitrary", "arbitrary"),
            vmem_limit_bytes=VMEM_LIMIT),
        name=name,
    )(flag, qT, k, vT)


_N_MIX_INPUTS = 16
_N_PROJ_INPUTS = 8


def _mix_mlp_kernel(*refs, n_ctx_tiles, n_tiles, token_major_out, fuse_next):
    (x_ref, og_ref, om_ref, pp_ref, pc_ref, pn_ref, gt1_ref, sh2_ref, sc2_ref,
     gt2_ref, gn_ref, ls_ref, wp_ref, wo_ref, w1_ref, w2_ref) = refs[:_N_MIX_INPUTS]
    n_in = _N_MIX_INPUTS + (_N_PROJ_INPUTS if fuse_next else 0)
    o_ref = refs[n_in]
    tm = x_ref.shape[2]
    wide = functools.partial(_wide, width=tm)
    ti = pl.program_id(1)

    n_og = og_ref.shape[1]
    n_om = om_ref.shape[1]
    half = wo_ref.shape[1] // 2
    y_lo = (_dot(wo_ref[0, :half, :n_og], og_ref[0])
            + _dot(wo_ref[0, :half, n_og:n_og + n_om], om_ref[0]))
    y_hi = (_dot(wo_ref[0, half:, :n_og], og_ref[0])
            + _dot(wo_ref[0, half:, n_og:n_og + n_om], om_ref[0]))

    cur = pc_ref[0]
    prev_ok = jnp.logical_and(ti != 0, ti != n_ctx_tiles)
    next_ok = jnp.logical_and(ti != n_ctx_tiles - 1, ti != n_tiles - 1)
    prev = jnp.where(prev_ok, pp_ref[0][:, tm - LANES:], 0.0)
    nxt = jnp.where(next_ok, pn_ref[0][:, :LANES], 0.0)
    ext = jnp.concatenate([prev, cur, nxt], axis=1)
    width = tm + 2 * LANES

    def shifted(a, s):
        return pltpu.roll(a, s % width, axis=1)

    in_ctx = ti < n_ctx_tiles
    seg_start = jnp.where(in_ctx, 0, n_ctx_tiles * tm)
    seg_len = jnp.where(in_ctx, n_ctx_tiles * tm, (n_tiles - n_ctx_tiles) * tm)
    pos = lax.broadcasted_iota(jnp.int32, (1, tm), 1) + (ti * tm - seg_start)
    gd = POOL_GROUP_DIM
    run = ext + shifted(ext, 1)
    outs = []
    for gi, w in enumerate(POOL_WINDOWS):
        if gi > 0:
            quarter = w // 4
            run = run[gd:]
            run = shifted(run, quarter) + shifted(run, -quarter)
        cnt = jnp.minimum(pos + w // 2, seg_len) - jnp.maximum(pos - w // 2, 0)
        mean = run[:gd, LANES:LANES + tm] / cnt.astype(F32)
        dlt = (mean - cur[gi * gd:(gi + 1) * gd]).astype(BF16)
        outs.append(_dot(wp_ref[0, gi], dlt))
    op = jnp.concatenate(outs, axis=0) * wide(ls_ref[0])

    op = op.astype(BF16)
    y = jnp.concatenate([y_lo + _dot(wo_ref[0, :half, n_og + n_om:], op),
                         y_hi + _dot(wo_ref[0, half:, n_og + n_om:], op)], axis=0)
    x1 =x_ref[0] + wide(gt1_ref[0, 0, 0]) * y

    d_model = x1.shape[0]
    xn = x1 * _rms_rows(x1, d_model) * wide(gn_ref[0])
    h2 = (xn * (1.0 + wide(sc2_ref[0, 0, 0])) + wide(sh2_ref[0, 0, 0])).astype(BF16)
    d_ff = w1_ref.shape[1]
    n_chunks = d_ff // FF_CHUNK

    def up(c):
        a = jnp.maximum(_dot(w1_ref[0, c * FF_CHUNK:(c + 1) * FF_CHUNK, :], h2), 0.0)
        return (a * a).astype(BF16)

    y2 = None
    hidden = up(0)
    for c in range(n_chunks):
        nxt = up(c + 1) if c + 1 < n_chunks else None
        part = _dot(w2_ref[0, :, c * FF_CHUNK:(c + 1) * FF_CHUNK], hidden)
        y2 = part if y2 is None else y2 + part
        hidden = nxt
    x2 = x1 + wide(gt2_ref[0, 0, 0]) * y2
    o_ref[0] = x2.T if token_major_out else x2
    if fuse_next:
        _proj_body(x2, *refs[_N_MIX_INPUTS:n_in], *refs[n_in + 1:])


def _mix_mlp_call(layer, xT, ogT, omT, poolT, modT, gn2, ls_rep, w_poolT, w_outT, w1T, w2T,
                  n_ctx_tiles, final, next_proj=None):
    b, d, t = xT.shape
    tm = TOKEN_TILE
    nt = t // tm
    nb = modT.shape[1] - 1

    def mod_spec(slot):
        return pl.BlockSpec((1, 1, 1, d, LANES),
                            lambda bi, ti: (layer, jnp.where(ti < n_ctx_tiles, nb, bi), slot, 0, 0))

    def const_spec(arr):
        return pl.BlockSpec((1,) + arr.shape[1:], lambda bi, ti: (layer,) + (0,) * (arr.ndim - 1))

    def tok_spec(rows):
        return pl.BlockSpec((1, rows, tm), lambda bi, ti: (bi, 0, ti))

    kern = functools.partial(_mix_mlp_kernel, n_ctx_tiles=n_ctx_tiles, n_tiles=nt,
                             token_major_out=final, fuse_next=next_proj is not None)
    p_in_specs, p_operands, p_out_specs, p_out_shapes = next_proj or ([], (), (), ())
    if final:
        out_spec = pl.BlockSpec((1, tm, d), lambda bi, ti: (bi, jnp.maximum(ti - n_ctx_tiles, 0), 0))
        out_shape = jax.ShapeDtypeStruct((b, t - n_ctx_tiles * tm, d), F32)
    else:
        out_spec = tok_spec(d)
        out_shape = jax.ShapeDtypeStruct((b, d, t), F32)
    return pl.pallas_call(
        kern,
        grid=(b, nt),
        in_specs=[
            tok_spec(d), tok_spec(ogT.shape[1]), tok_spec(omT.shape[1]),
            pl.BlockSpec((1, POOL_WIDTH, tm), lambda bi, ti: (bi, 0, jnp.maximum(ti - 1, 0))),
            tok_spec(POOL_WIDTH),
            pl.BlockSpec((1, POOL_WIDTH, tm), lambda bi, ti: (bi, 0, jnp.minimum(ti + 1, nt - 1))),
            mod_spec(2), mod_spec(3), mod_spec(4), mod_spec(5),
            const_spec(gn2), const_spec(ls_rep), const_spec(w_poolT), const_spec(w_outT),
            const_spec(w1T), const_spec(w2T),
        ] + list(p_in_specs),
        out_specs=(out_spec,) + tuple(p_out_specs),
        out_shape=(out_shape,) + tuple(p_out_shapes),
        compiler_params=pltpu.CompilerParams(
            dimension_semantics=("arbitrary", "arbitrary"), vmem_limit_bytes=VMEM_LIMIT),
        name="mix_mlp_proj" if next_proj else "mix_mlp",
    )(xT, ogT, omT, poolT, poolT, poolT, modT, modT, modT, modT, gn2, ls_rep, w_poolT, w_outT,
      w1T, w2T, *p_operands)


def _rope_table(seq_len, ctx_len):
    rows = seq_len // GRID_W
    row = jnp.repeat(jnp.arange(rows, dtype=F32), GRID_W)
    col = jnp.tile(jnp.arange(GRID_W, dtype=F32), rows)

    def block(rot_dim):
        n = rot_dim // 4
        inv = ROPE_THETA ** (-jnp.arange(n, dtype=F32) / n)
        ang = jnp.concatenate([row[:, None] * inv, col[:, None] * inv], axis=-1)
        cos, sin = jnp.cos(ang).T, jnp.sin(ang).T
        c = jnp.concatenate([cos, cos], axis=0)
        s = jnp.concatenate([-sin, sin], axis=0)
        c = jnp.concatenate([jnp.ones((rot_dim, ctx_len), F32), c], axis=1)
        s = jnp.concatenate([jnp.zeros((rot_dim, ctx_len), F32), s], axis=1)
        return c, s

    cg, sg = block(HEAD_DIM)
    cm, sm = block(MLA_ROPE)
    return jnp.concatenate([cg, sg, cm, sm], axis=0)


def _lane_rep(a):
    return jnp.broadcast_to(a[..., None], a.shape + (LANES,)).astype(F32)


def _swap_half(n):
    return (np.arange(n) + n // 2) % n


def _prep_weights(w_mod, b_mod, g_norm1, g_norm2, w_in, g_q_gqa, g_k_gqa, g_cq, g_ckv, w_uq, w_ukv,
                  g_q_mla, g_k_mla, w_pool, ls_pool, w_out, w_mlp1, w_mlp2):
    o_cq = (GQA_HEADS + 2 * GQA_KV_HEADS) * HEAD_DIM
    o_pool = o_cq + MLA_Q_RANK + MLA_KV_RANK + MLA_ROPE
    sw64 = _swap_half(HEAD_DIM)
    sw32 = _swap_half(MLA_ROPE)
    w_in_ord = jnp.concatenate([w_in[:, :, o_cq:o_pool], w_in[:, :, :o_cq], w_in[:, :, o_pool:]], axis=2)
    assert w_in_ord.shape[2] == _IN_EXT
    w_inT = jnp.transpose(w_in_ord, (0, 2, 1)).astype(BF16)
    w_uqT = jnp.transpose(w_uq, (0, 2, 1)).astype(BF16)
    w_ukvT = jnp.transpose(w_ukv, (0, 2, 1)).astype(BF16)

    gq_r, gk_r = g_q_mla[:, MLA_NOPE:], g_k_mla[:, MLA_NOPE:]
    gpack = jnp.concatenate([
        g_q_gqa, g_q_gqa[:, sw64], g_k_gqa, g_k_gqa[:, sw64], g_cq, g_ckv,
        g_q_mla[:, :MLA_NOPE], gq_r, gq_r[:, sw32], g_k_mla[:, :MLA_NOPE], gk_r, gk_r[:, sw32]], axis=1)
    assert gpack.shape[1] == _G_ROWS
    return dict(
        w_modT=jnp.transpose(w_mod, (0, 2, 1)).astype(BF16),
        b_rep=_lane_rep(b_mod),
        gn1=_lane_rep(g_norm1), gn2=_lane_rep(g_norm2),
        w_inT=w_inT, w_uqT=w_uqT, w_ukvT=w_ukvT, gpack=_lane_rep(gpack),
        w_poolT=jnp.transpose(w_pool, (0, 1, 3, 2)).astype(BF16),
        ls_rep=_lane_rep(ls_pool),
        w_outT=jnp.transpose(w_out, (0, 2, 1)).astype(BF16),
        w1T=jnp.transpose(w_mlp1, (0, 2, 1)).astype(BF16),
        w2T=jnp.transpose(w_mlp2, (0, 2, 1)).astype(BF16),
    )


def kernel(x, c, ctx, c_ctx, w_mod, b_mod, g_norm1, g_norm2, w_in, g_q_gqa, g_k_gqa, g_cq, g_ckv, w_uq, w_ukv, g_q_mla, g_k_mla, w_pool, ls_pool, w_out, w_mlp1, w_mlp2):
    batch, seq, d_model = x.shape
    ctx_len = ctx.shape[1]
    depth = w_in.shape[0]
    assert ctx_len % TOKEN_TILE == 0 and seq % TOKEN_TILE == 0 and seq % GRID_W == 0
    assert TOKEN_TILE == Q_TILE
    assert ctx_len % KV_CHUNK_ONLINE == 0 and seq % KV_CHUNK_ONLINE == 0
    assert GQA_HEADS % HEADS_PER_STEP == 0 and HEADS_PER_STEP % GQA_GROUP == 0
    assert MLA_HEADS % HEADS_PER_STEP == 0
    n_ctx_tiles = ctx_len // TOKEN_TILE

    def bounded_flag(g_q, g_k, dim):
        bound = dim ** 0.5 * LOG2E * jnp.max(jnp.abs(g_q), axis=1) * jnp.max(jnp.abs(g_k), axis=1)
        return (bound <= SAFE_LOG2_BOUND).astype(jnp.int32)

    flag_g = bounded_flag(g_q_gqa, g_k_gqa, HEAD_DIM)
    flag_m = bounded_flag(g_q_mla, g_k_mla, MLA_QK)

    p = _prep_weights(w_mod, b_mod, g_norm1, g_norm2, w_in, g_q_gqa, g_k_gqa, g_cq, g_ckv, w_uq,
                      w_ukv, g_q_mla, g_k_mla, w_pool, ls_pool, w_out, w_mlp1, w_mlp2)
    ropeT = _rope_table(seq, ctx_len)
    c_rep = _lane_rep(jnp.concatenate([c, c_ctx[None, :]], axis=0))
    modT = _mod_call(p["w_modT"], c_rep, p["b_rep"])

    xT = _to_feature_major_call(ctx, x, n_ctx_tiles)
    t_all = xT.shape[2]
    proj_weights = (modT, p["gn1"], p["w_inT"], p["gpack"], ropeT, p["w_uqT"], p["w_ukvT"], n_ctx_tiles)
    qg, kg, vg, qm, km, vm, pool_in = _proj_call(0, xT, *proj_weights)
    for i in range(depth):
        last = i == depth - 1
        ogT = _attn_call(flag_g[i:i + 1], qg, kg, vg, GQA_GROUP, True, n_ctx_tiles, ctx_len,
                         "attn_gqa")
        omT = _attn_call(flag_m[i:i + 1], qm, km, vm, 1, False, n_ctx_tiles, ctx_len, "attn_mla")
        next_proj = None if last else _proj_io(i + 1, batch, d_model, t_all, *proj_weights)
        outs = _mix_mlp_call(i, xT, ogT, omT, pool_in, modT, p["gn2"], p["ls_rep"], p["w_poolT"],
                             p["w_outT"], p["w1T"], p["w2T"], n_ctx_tiles, final=last,
                             next_proj=next_proj)
        xT = outs[0]
        if not last:
            qg, kg, vg, qm, km, vm, pool_in = outs[1:]
    return xT
```
